```python
import jax, jax.numpy as jnp
from jax import lax
import numpy as np

D_MODEL = 2048
BATCH = 4
SEQ = 4096
DEPTH = 2
DEC_BATCH = 8
DEC_SEQ = 32
PAST_LEN = 2048

CHUNK = 64
N_EVEN = (DEPTH + 1) // 2
N_ODD = DEPTH // 2
H_A = 4
DK_A = D_MODEL // 16
DV_A = D_MODEL // 8
GATE_RANK = 16
GATE_TEMP = 16.0
H_B = 8
DH_B = D_MODEL // 16
N_PREV_CHUNKS = 8
BAND_PAST = N_PREV_CHUNKS * CHUNK
BAND = BAND_PAST + CHUNK
REL_CLIP = 128
CHUNK_C = 128
DC = D_MODEL
G_C = 8
D_FF = ((8 * D_MODEL // 3 + 127) // 128) * 128
CONV_W = 3
ALPHA = (2 * DEPTH) ** 0.25
BETA = (8 * DEPTH) ** -0.25
LN_EPS = 1e-5
NEG_INF = -1e30
IN_EVEN = 2 * H_A * DK_A + 2 * H_A * DV_A + GATE_RANK + 3 * H_B * DH_B
MIX_EVEN = H_A * DV_A + H_B * DH_B

kernel_name = 'hybrid_streaming_gla_band_gmlp_step'


def _layer_norm(x, g, b):
    xf = x.astype(jnp.float32)
    mu = jnp.mean(xf, -1, keepdims=True)
    xc = xf - mu
    var = jnp.mean(xc * xc, -1, keepdims=True)
    y = xc * lax.rsqrt(var + LN_EPS)
    return (y * g.astype(jnp.float32) + b.astype(jnp.float32)).astype(x.dtype)


def _gla_chunked(q, k, v, logg, s0):
    B, T, H, K = q.shape
    V = v.shape[-1]
    L = min(T, CHUNK)
    n = T // L

    def to_blocks(a):
        return jnp.moveaxis(a.reshape(B, n, L, H, a.shape[-1]), 1, 0)

    causal = jnp.tril(jnp.ones((L, L), dtype=bool))

    def step(S, xs):
        qc, kc, vc, gc = xs
        b = jnp.cumsum(gc, axis=1)
        q_dec = qc * jnp.exp(b)
        k_dec = kc * jnp.exp(-b)
        a = jnp.where(causal, jnp.einsum('bthk,bshk->bhts', q_dec, k_dec), 0.0)
        o = jnp.einsum('bthk,bhkv->bthv', q_dec, S) + jnp.einsum('bhts,bshv->bthv', a, vc)
        b_last = b[:, -1]
        k_upd = kc * jnp.exp(b_last[:, None] - b)
        S = S * jnp.exp(b_last)[..., None] + jnp.einsum('bshk,bshv->bhkv', k_upd, vc)
        return S, o

    s_fin, o = lax.scan(step, s0, (to_blocks(q), to_blocks(k), to_blocks(v), to_blocks(logg)))
    return jnp.moveaxis(o, 0, 1).reshape(B, T, H, V), s_fin


def _rel_bias(table, rel):
    return table[:, jnp.clip(rel, -REL_CLIP, REL_CLIP) + REL_CLIP].astype(jnp.float32)


def _band_attn_prompt(q, k, v, table):
    B, T, H, Dh = q.shape
    n = T // CHUNK
    pad = ((0, 0), (BAND_PAST, 0), (0, 0), (0, 0))
    kp = jnp.pad(k, pad)
    vp = jnp.pad(v, pad)
    r = jnp.arange(CHUNK)
    j = jnp.arange(BAND)
    bias = _rel_bias(table, r[:, None] + BAND_PAST - j[None, :])
    scale = DH_B ** -0.5

    def one_chunk(c):
        start = c * CHUNK
        qb = lax.dynamic_slice_in_dim(q, start, CHUNK, axis=1)
        kb = lax.dynamic_slice_in_dim(kp, start, BAND, axis=1)
        vb = lax.dynamic_slice_in_dim(vp, start, BAND, axis=1)
        s = jnp.einsum('bqhd,bkhd->bhqk', qb, kb).astype(jnp.float32) * scale + bias
        valid = (start - BAND_PAST + j) >= 0
        s = jnp.where(valid[None, None, None, :], s, NEG_INF)
        p = jax.nn.softmax(s, axis=-1)
        return jnp.einsum('bhqk,bkhd->bqhd', p.astype(v.dtype), vb)

    o = lax.map(one_chunk, jnp.arange(n))
    return jnp.moveaxis(o, 0, 1).reshape(B, T, H, Dh)


def _band_attn_step(q, k, v, ck, cv, table):
    T = q.shape[1]
    L = ck.shape[1]
    kk = jnp.concatenate([ck.astype(k.dtype), k], axis=1)
    vv = jnp.concatenate([cv.astype(v.dtype), v], axis=1)
    qpos = PAST_LEN + jnp.arange(T)
    kpos = jnp.concatenate([PAST_LEN - L + jnp.arange(L), qpos])
    bias = _rel_bias(table, qpos[:, None] - kpos[None, :])
    s = jnp.einsum('bqhd,bkhd->bhqk', q, kk).astype(jnp.float32) * (DH_B ** -0.5) + bias
    p = jax.nn.softmax(s, axis=-1)
    return jnp.einsum('bhqk,bkhd->bqhd', p.astype(v.dtype), vv)


def _even_mixer(x, P, e, state):
    B, T, _ = x.shape
    f32 = jnp.float32
    sizes = (H_A * DK_A, H_A * DK_A, H_A * DV_A, GATE_RANK, H_A * DV_A,
             H_B * DH_B, H_B * DH_B, H_B * DH_B)
    split_at = [int(s) for s in np.cumsum(sizes)[:-1]]
    proj = x @ P['w_in_even'][e]
    q_a, k_a, v_a, g_low, og, q_b, k_b, v_b = jnp.split(proj, split_at, axis=-1)
    q_a = (q_a.reshape(B, T, H_A, DK_A) * (DK_A ** -0.5)).astype(f32)
    k_a = k_a.reshape(B, T, H_A, DK_A).astype(f32)
    v_a = v_a.reshape(B, T, H_A, DV_A).astype(f32)
    logg = jax.nn.log_sigmoid((g_low @ P['w_gate_up'][e] + P['b_gate'][e]).astype(f32)) / GATE_TEMP
    logg = logg.reshape(B, T, H_A, DK_A)
    s0 = jnp.zeros((B, H_A, DK_A, DV_A), f32) if state is None else state[0].astype(f32)
    o_a, s_new = _gla_chunked(q_a, k_a, v_a, logg, s0)
    o_a = o_a * lax.rsqrt(jnp.mean(o_a * o_a, -1, keepdims=True) + LN_EPS) * P['gla_norm_g'][e].astype(f32)
    o_a = (o_a.reshape(B, T, H_A * DV_A) * jax.nn.silu(og.astype(f32))).astype(x.dtype)
    q_b = q_b.reshape(B, T, H_B, DH_B)
    k_b = k_b.reshape(B, T, H_B, DH_B)
    v_b = v_b.reshape(B, T, H_B, DH_B)
    if state is None:
        o_b = _band_attn_prompt(q_b, k_b, v_b, P['rel_bias'][e])
        keep = min(BAND_PAST, T)
        k_rows, v_rows = k_b[:, T - keep:], v_b[:, T - keep:]
    else:
        o_b = _band_attn_step(q_b, k_b, v_b, state[1], state[2], P['rel_bias'][e])
        k_rows, v_rows = k_b, v_b
    mixed = jnp.concatenate([o_a, o_b.reshape(B, T, H_B * DH_B)], axis=-1)
    return mixed @ P['w_out_even'][e], k_rows, v_rows, s_new.astype(x.dtype)


def _odd_mixer(x, P, o):
    B, T, _ = x.shape
    h = jax.nn.gelu(x @ P['w_in_odd'][o])
    u, v = jnp.split(h, 2, axis=-1)
    v = _layer_norm(v, P['ln_v_g'][o], P['ln_v_b'][o])
    L = min(T, CHUNK_C)
    n = T // L
    causal = jnp.tril(jnp.ones((L, L), dtype=bool))
    wm = jnp.where(causal, P['w_spatial'][o][:, :L, :L], 0.0)
    vc = v.reshape(B, n, L, G_C, DC // G_C)
    bias = jnp.transpose(P['b_spatial'][o][:, :L])[None, None, :, :, None]
    sv = jnp.einsum('gts,bnsgc->bntgc', wm.astype(v.dtype), vc) + bias
    y = (u * sv.reshape(B, T, DC)) @ P['w_out_odd'][o]
    return y, v


def _conv_ffn(x, P, i, conv_state):
    B, T, _ = x.shape
    u = x @ P['ffn_w1'][i]
    z = x @ P['ffn_w2'][i]
    hist = jnp.zeros((B, CONV_W - 1, D_FF), u.dtype) if conv_state is None else conv_state.astype(u.dtype)
    upad = jnp.concatenate([hist, u], axis=1)
    cw = P['ffn_conv_w'][i]
    c = upad[:, 0:T] * cw[0]
    for tap in range(1, CONV_W):
        c = c + upad[:, tap:tap + T] * cw[tap]
    hdn = jax.nn.gelu(c + P['ffn_conv_b'][i]) * z
    return hdn @ P['ffn_w3'][i], upad[:, -(CONV_W - 1):]


def _trunk(x, P, state):
    ks, vs, gs, cs, ms = [], [], [], [], []
    for i in range(DEPTH):
        if i % 2 == 0:
            e = i // 2
            st = None if state is None else (state[2][e], state[0][e], state[1][e])
            h, kr, vr, s_new = _even_mixer(x, P, e, st)
            ks.append(kr)
            vs.append(vr)
            gs.append(s_new)
        else:
            h, v_rows = _odd_mixer(x, P, i // 2)
            ms.append(v_rows)
        x = _layer_norm(ALPHA * x + h, P['ln1_g'][i], P['ln1_b'][i])
        f, c_new = _conv_ffn(x, P, i, None if state is None else state[3][i])
        cs.append(c_new)
        x = _layer_norm(ALPHA * x + f, P['ln2_g'][i], P['ln2_b'][i])
    return x, jnp.stack(ks), jnp.stack(vs), jnp.stack(gs), jnp.stack(cs), jnp.stack(ms)


def setup_inputs(seed: int = 0) -> dict:
    key = jax.random.key(seed)
    k = jax.random.split(key, 32)

    def nrm(kk, shape, s):
        return jax.random.normal(kk, shape, jnp.float32) * s

    lb = min(BAND_PAST, PAST_LEN)
    return {
        'x_prompt': nrm(k[0], (BATCH, SEQ, D_MODEL), 1.0),
        'x_sample': nrm(k[1], (DEC_BATCH, DEC_SEQ, D_MODEL), 1.0),
        'cache_attn_k': nrm(k[2], (N_EVEN, DEC_BATCH, lb, H_B, DH_B), 1.0),
        'cache_attn_v': nrm(k[3], (N_EVEN, DEC_BATCH, lb, H_B, DH_B), 1.0),
        'state_gla': nrm(k[4], (N_EVEN, DEC_BATCH, H_A, DK_A, DV_A), 1.0),
        'state_ffn_conv': nrm(k[5], (DEPTH, DEC_BATCH, CONV_W - 1, D_FF), 1.0),
        'w_in_even': nrm(k[6], (N_EVEN, D_MODEL, IN_EVEN), D_MODEL ** -0.5),
        'w_gate_up': nrm(k[7], (N_EVEN, GATE_RANK, H_A * DK_A), GATE_RANK ** -0.5),
        'b_gate': nrm(k[8], (N_EVEN, H_A * DK_A), 0.1),
        'gla_norm_g': 1.0 + nrm(k[9], (N_EVEN, H_A, DV_A), 0.02),
        'rel_bias': nrm(k[10], (N_EVEN, H_B, 2 * REL_CLIP + 1), 0.1),
        'w_out_even': nrm(k[11], (N_EVEN, MIX_EVEN, D_MODEL), BETA * MIX_EVEN ** -0.5),
        'w_in_odd': nrm(k[12], (N_ODD, D_MODEL, 2 * DC), D_MODEL ** -0.5),
        'ln_v_g': 1.0 + nrm(k[13], (N_ODD, DC), 0.02),
        'ln_v_b': nrm(k[14], (N_ODD, DC), 0.02),
        'w_spatial': nrm(k[15], (N_ODD, G_C, CHUNK_C, CHUNK_C), CHUNK_C ** -0.5),
        'b_spatial': 1.0 + nrm(k[16], (N_ODD, G_C, CHUNK_C), 0.1),
        'w_out_odd': nrm(k[17], (N_ODD, DC, D_MODEL), BETA * DC ** -0.5),
        'ffn_w1': nrm(k[18], (DEPTH, D_MODEL, D_FF), D_MODEL ** -0.5),
        'ffn_w2': nrm(k[19], (DEPTH, D_MODEL, D_FF), D_MODEL ** -0.5),
        'ffn_conv_w': nrm(k[20], (DEPTH, CONV_W, D_FF), CONV_W ** -0.5),
        'ffn_conv_b': nrm(k[21], (DEPTH, D_FF), 0.02),
        'ffn_w3': nrm(k[22], (DEPTH, D_FF, D_MODEL), BETA * D_FF ** -0.5),
        'ln1_g': 1.0 + nrm(k[23], (DEPTH, D_MODEL), 0.02),
        'ln1_b': nrm(k[24], (DEPTH, D_MODEL), 0.02),
        'ln2_g': 1.0 + nrm(k[25], (DEPTH, D_MODEL), 0.02),
        'ln2_b': nrm(k[26], (DEPTH, D_MODEL), 0.02),
    }


def reference(x_prompt, x_sample, cache_attn_k, cache_attn_v, state_gla, state_ffn_conv,
              w_in_even, w_gate_up, b_gate, gla_norm_g, rel_bias, w_out_even,
              w_in_odd, ln_v_g, ln_v_b, w_spatial, b_spatial, w_out_odd,
              ffn_w1, ffn_w2, ffn_conv_w, ffn_conv_b, ffn_w3,
              ln1_g, ln1_b, ln2_g, ln2_b):
    P = dict(w_in_even=w_in_even, w_gate_up=w_gate_up, b_gate=b_gate, gla_norm_g=gla_norm_g,
             rel_bias=rel_bias, w_out_even=w_out_even, w_in_odd=w_in_odd, ln_v_g=ln_v_g,
             ln_v_b=ln_v_b, w_spatial=w_spatial, b_spatial=b_spatial, w_out_odd=w_out_odd,
             ffn_w1=ffn_w1, ffn_w2=ffn_w2, ffn_conv_w=ffn_conv_w, ffn_conv_b=ffn_conv_b,
             ffn_w3=ffn_w3, ln1_g=ln1_g, ln1_b=ln1_b, ln2_g=ln2_g, ln2_b=ln2_b)
    y_prompt, k_p, v_p, gla_p, conv_p, _ = _trunk(x_prompt, P, None)
    y_sample, k_s, v_s, gla_s, conv_s, mlp_v_s = _trunk(
        x_sample, P, (cache_attn_k, cache_attn_v, state_gla, state_ffn_conv))
    return (y_prompt, y_sample, k_p, v_p, gla_p, conv_p, k_s, v_s, gla_s, conv_s, mlp_v_s)
```

```python
import functools

import numpy as np
import jax
import jax.numpy as jnp
from jax import lax
from jax.experimental import pallas as pl
from jax.experimental.pallas import tpu as pltpu

F32 = jnp.float32
BF16 = jnp.bfloat16

D_MODEL = 2048
DEPTH = 2
CHUNK = 64
H_A = 4
DK_A = D_MODEL // 16
DV_A = D_MODEL // 8
GATE_RANK = 16
GATE_TEMP = 16.0
H_B = 8
DH_B = D_MODEL // 16
N_PREV_CHUNKS = 8
BAND_PAST = N_PREV_CHUNKS * CHUNK
BAND = BAND_PAST + CHUNK
REL_CLIP = 128
CHUNK_C = 128
DC = D_MODEL
G_C = 8
D_FF = ((8 * D_MODEL // 3 + 127) // 128) * 128
CONV_W = 3
ALPHA = (2 * DEPTH) ** 0.25
LN_EPS = 1e-5
NEG_INF = -1e30
PAST_LEN = 2048

LANE = 128
SUBLANE = 8
BF16_ROWS = 16
FF_TILE = 512
D_FF_PAD = ((D_FF + FF_TILE - 1) // FF_TILE) * FF_TILE
N_FF_TILES = D_FF_PAD // FF_TILE
GATE_PAD = LANE
QKG_W = 2 * H_A * DK_A + GATE_PAD
VMEM_LIMIT = 56 * 1024 * 1024


def _cparams(n_axes, vmem=VMEM_LIMIT, **kw):
    return pltpu.CompilerParams(dimension_semantics=("arbitrary",) * n_axes,
                                vmem_limit_bytes=vmem, **kw)


def _layer_norm(xf, g, b):
    mu = jnp.mean(xf, -1, keepdims=True)
    xc = xf - mu
    var = jnp.mean(xc * xc, -1, keepdims=True)
    y = xc * lax.rsqrt(var + LN_EPS)
    return y * g + b


def _log_sigmoid(x):
    return jnp.minimum(x, 0.0) - jnp.log1p(jnp.exp(-jnp.abs(x)))


def _linear_kernel(x_ref, w_ref, o_ref, *maybe_xb_ref, act):
    xb = x_ref[...].astype(BF16)
    y = jnp.dot(xb, w_ref[...], preferred_element_type=F32)
    if act == "gelu":
        y = jax.nn.gelu(y)
    o_ref[...] = y.astype(o_ref.dtype)
    if maybe_xb_ref:
        maybe_xb_ref[0][...] = xb


def _linear(x, w, *, out_dtype, tm, tn, act=None, emit_xb=False, x_block_index=None, n_row_blocks=None):
    m, k = x.shape
    n = w.shape[1]
    n_rows = (m // tm) if n_row_blocks is None else n_row_blocks
    xmap = (lambda i, j: (i, 0)) if x_block_index is None else (lambda i, j: (x_block_index(i), 0))
    out_shape = [jax.ShapeDtypeStruct((n_rows * tm, n), out_dtype)]
    out_specs = [pl.BlockSpec((tm, tn), lambda i, j: (i, j))]
    if emit_xb:
        assert n == tn
        out_shape.append(jax.ShapeDtypeStruct((m, k), BF16))
        out_specs.append(pl.BlockSpec((tm, k), lambda i, j: (i, 0)))
    res = pl.pallas_call(
        functools.partial(_linear_kernel, act=act),
        grid=(n_rows, n // tn),
        in_specs=[pl.BlockSpec((tm, k), xmap), pl.BlockSpec((k, tn), lambda i, j: (0, j))],
        out_specs=out_specs,
        out_shape=out_shape,
        compiler_params=_cparams(2),
        name="linear",
    )(x, w)
    return res if emit_xb else res[0]


def _proj_ln_kernel(*refs, n_in, has_res, pre_gelu, emit_bf16):
    xs = refs[:n_in]
    ws = refs[n_in:2 * n_in]
    pos = 2 * n_in
    res_ref = refs[pos] if has_res else None
    pos += int(has_res)
    g_ref, b_ref = refs[pos], refs[pos + 1]
    outs = refs[pos + 2:]
    h = jnp.dot(xs[0][...].astype(BF16), ws[0][...], preferred_element_type=F32)
    for x_ref, w_ref in zip(xs[1:], ws[1:]):
        h = h + jnp.dot(x_ref[...].astype(BF16), w_ref[...], preferred_element_type=F32)
    if pre_gelu:
        h = jax.nn.gelu(h)
    if has_res:
        h = ALPHA * res_ref[...] + h
    y = _layer_norm(h, g_ref[...], b_ref[...])
    outs[0][...] = y
    if emit_bf16:
        outs[1][...] = y.astype(BF16)


def _proj_ln(xs, ws, res, g, b, *, tm, pre_gelu=False, emit_bf16=True):
    m = xs[0].shape[0]
    n = ws[0].shape[1]
    n_in = len(xs)
    in_specs = [pl.BlockSpec((tm, x.shape[1]), lambda i: (i, 0)) for x in xs]
    in_specs += [pl.BlockSpec(w.shape, lambda i: (0, 0)) for w in ws]
    args = list(xs) + list(ws)
    if res is not None:
        in_specs.append(pl.BlockSpec((tm, n), lambda i: (i, 0)))
        args.append(res)
    in_specs += [pl.BlockSpec((1, n), lambda i: (0, 0))] * 2
    args += [g.reshape(1, n), b.reshape(1, n)]
    out_shape = [jax.ShapeDtypeStruct((m, n), F32)]
    out_specs = [pl.BlockSpec((tm, n), lambda i: (i, 0))]
    if emit_bf16:
        out_shape.append(jax.ShapeDtypeStruct((m, n), BF16))
        out_specs.append(pl.BlockSpec((tm, n), lambda i: (i, 0)))
    return pl.pallas_call(
        functools.partial(_proj_ln_kernel, n_in=n_in, has_res=res is not None,
                          pre_gelu=pre_gelu, emit_bf16=emit_bf16),
        grid=(m // tm,),
        in_specs=in_specs, out_specs=out_specs, out_shape=out_shape,
        compiler_params=_cparams(1),
        name="proj_ln",
    )(*args)


def _gla_kernel(qkg_ref, v_ref, og_ref, wup_ref, bg_ref, s0_ref, gn_ref,
                o_ref, sfin_ref, st_ref, *, L, n_chunks):
    c = pl.program_id(1)

    @pl.when(c == 0)
    def _():
        st_ref[...] = s0_ref[0]

    hk = H_A * DK_A
    glow = qkg_ref[:, 2 * hk:2 * hk + GATE_PAD].astype(BF16)
    gate = jnp.dot(glow, wup_ref[...], preferred_element_type=F32) + bg_ref[...]
    logg = _log_sigmoid(gate) * (1.0 / GATE_TEMP)

    row = lax.broadcasted_iota(jnp.int32, (L, L), 0)
    col = lax.broadcasted_iota(jnp.int32, (L, L), 1)
    causal = row >= col
    tri = jnp.where(causal, 1.0, 0.0).astype(BF16)
    hi = logg.astype(BF16)
    r1 = logg - hi.astype(F32)
    mid = r1.astype(BF16)
    lo = (r1 - mid.astype(F32)).astype(BF16)
    bcum = (jnp.dot(tri, hi, preferred_element_type=F32)
            + jnp.dot(tri, mid, preferred_element_type=F32)
            + jnp.dot(tri, lo, preferred_element_type=F32))

    nt = (((1,), (1,)), ((), ()))
    tn_dims = (((0,), (0,)), ((), ()))
    for h in range(H_A):
        ks = slice(h * DK_A, (h + 1) * DK_A)
        vs = slice(h * DV_A, (h + 1) * DV_A)
        bh = bcum[:, ks]
        qh = qkg_ref[:, h * DK_A:(h + 1) * DK_A]
        kh = qkg_ref[:, hk + h * DK_A:hk + (h + 1) * DK_A]
        q_dec = ((qh * (DK_A ** -0.5)) * jnp.exp(bh)).astype(BF16)
        k_dec = (kh * jnp.exp(-bh)).astype(BF16)
        a = lax.dot_general(q_dec, k_dec, nt, preferred_element_type=F32)
        a = jnp.where(causal, a, 0.0).astype(BF16)
        vh = v_ref[:, vs]
        st = st_ref[h]
        o = (lax.dot_general(q_dec, st.astype(BF16), nt, preferred_element_type=F32)
             + jnp.dot(a, vh, preferred_element_type=F32))
        b_last = bh[L - 1:L, :]
        k_upd = (kh * jnp.exp(b_last - bh)).astype(BF16)
        st_ref[h] = st * jnp.exp(b_last) + lax.dot_general(
            vh, k_upd, tn_dims, preferred_element_type=F32)
        on = o * lax.rsqrt(jnp.mean(o * o, -1, keepdims=True) + LN_EPS) * gn_ref[h:h + 1, :]
        o_ref[:, vs] = (on * jax.nn.silu(og_ref[:, vs])).astype(BF16)

    @pl.when(c == n_chunks - 1)
    def _():
        sfin_ref[0] = st_ref[...]


def _gla(qkg, v_a, og, wup, bg, s0_t, gn, *, B, T):
    L = min(T, CHUNK)
    n = T // L
    hv = H_A * DV_A
    row_map = lambda b, c: (b * n + c, 0)
    const2 = lambda b, c: (0, 0)
    return pl.pallas_call(
        functools.partial(_gla_kernel, L=L, n_chunks=n),
        grid=(B, n),
        in_specs=[
            pl.BlockSpec((L, QKG_W), row_map),
            pl.BlockSpec((L, hv), row_map),
            pl.BlockSpec((L, hv), row_map),
            pl.BlockSpec(wup.shape, const2),
            pl.BlockSpec(bg.shape, const2),
            pl.BlockSpec((1, H_A, DV_A, DK_A), lambda b, c: (b, 0, 0, 0)),
            pl.BlockSpec(gn.shape, const2),
        ],
        out_specs=[
            pl.BlockSpec((L, hv), row_map),
            pl.BlockSpec((1, H_A, DV_A, DK_A), lambda b, c: (b, 0, 0, 0)),
        ],
        out_shape=[
            jax.ShapeDtypeStruct((B * T, hv), BF16),
            jax.ShapeDtypeStruct((B, H_A, DV_A, DK_A), F32),
        ],
        scratch_shapes=[pltpu.VMEM((H_A, DV_A, DK_A), F32)],
        compiler_params=_cparams(2),
        name="gla",
    )(qkg, v_a, og, wup, bg, s0_t, gn)


def _band_attn_kernel(q_ref, kv_ref, bias_ref, o_ref):
    c = pl.program_id(1)
    start = pl.multiple_of(jnp.maximum(c - N_PREV_CHUNKS, 0) * CHUNK, CHUNK)
    hd = H_B * DH_B
    nt = (((1,), (1,)), ((), ()))
    for h in range(H_B):
        hs = slice(h * DH_B, (h + 1) * DH_B)
        q = q_ref[:, hs]
        k = kv_ref[pl.ds(start, BAND), h * DH_B:(h + 1) * DH_B]
        v = kv_ref[pl.ds(start, BAND), hd + h * DH_B:hd + (h + 1) * DH_B]
        s = lax.dot_general(q, k, nt, preferred_element_type=F32) * (DH_B ** -0.5) + bias_ref[0, h]
        m = jnp.max(s, -1, keepdims=True)
        e = jnp.exp(s - m)
        p = e / jnp.sum(e, -1, keepdims=True)
        o_ref[:, hs] = jnp.dot(p.astype(BF16), v, preferred_element_type=F32).astype(BF16)


def _band_attn(q_b, kv_b, bias_all, *, B, T):
    n = T // CHUNK
    hd = H_B * DH_B
    return pl.pallas_call(
        _band_attn_kernel,
        grid=(B, n),
        in_specs=[
            pl.BlockSpec((CHUNK, hd), lambda b, c: (b * n + c, 0)),
            pl.BlockSpec((T, 2 * hd), lambda b, c: (b, 0)),
            pl.BlockSpec((1, H_B, CHUNK, BAND), lambda b, c: (jnp.minimum(c, N_PREV_CHUNKS), 0, 0, 0)),
        ],
        out_specs=pl.BlockSpec((CHUNK, hd), lambda b, c: (b * n + c, 0)),
        out_shape=jax.ShapeDtypeStruct((B * T, hd), BF16),
        compiler_params=_cparams(2),
        name="band_attn",
    )(q_b, kv_b, bias_all)


def _attn_step_kernel(q_ref, kv_ref, ck_ref, cv_ref, bias_c_ref, bias_n_ref, o_ref):
    hd = H_B * DH_B
    nt = (((1,), (1,)), ((), ()))
    scale = DH_B ** -0.5
    for h in range(H_B):
        hs = slice(h * DH_B, (h + 1) * DH_B)
        q = q_ref[:, hs].astype(BF16)
        kc = ck_ref[0, :, hs].astype(BF16)
        vc = cv_ref[0, :, hs].astype(BF16)
        kn = kv_ref[:, hs].astype(BF16)
        vn = kv_ref[:, hd + h * DH_B:hd + (h + 1) * DH_B].astype(BF16)
        sc = lax.dot_general(q, kc, nt, preferred_element_type=F32) * scale + bias_c_ref[h]
        sn = lax.dot_general(q, kn, nt, preferred_element_type=F32) * scale + bias_n_ref[h]
        m = jnp.maximum(jnp.max(sc, -1, keepdims=True), jnp.max(sn, -1, keepdims=True))
        ec = jnp.exp(sc - m)
        en = jnp.exp(sn - m)
        den = jnp.sum(ec, -1, keepdims=True) + jnp.sum(en, -1, keepdims=True)
        o = (jnp.dot((ec / den).astype(BF16), vc, preferred_element_type=F32)
             + jnp.dot((en / den).astype(BF16), vn, preferred_element_type=F32))
        o_ref[:, hs] = o.astype(BF16)


def _attn_step(q_b, kv_b, ck, cv, bias_c, bias_n, *, B, T):
    hd = H_B * DH_B
    lc = ck.shape[1]
    return pl.pallas_call(
        _attn_step_kernel,
        grid=(B,),
        in_specs=[
            pl.BlockSpec((T, hd), lambda b: (b, 0)),
            pl.BlockSpec((T, 2 * hd), lambda b: (b, 0)),
            pl.BlockSpec((1, lc, hd), lambda b: (b, 0, 0)),
            pl.BlockSpec((1, lc, hd), lambda b: (b, 0, 0)),
            pl.BlockSpec(bias_c.shape, lambda b: (0, 0, 0)),
            pl.BlockSpec(bias_n.shape, lambda b: (0, 0, 0)),
        ],
        out_specs=pl.BlockSpec((T, hd), lambda b: (b, 0)),
        out_shape=jax.ShapeDtypeStruct((B * T, hd), BF16),
        compiler_params=_cparams(1),
        name="attn_step",
    )(q_b, kv_b, ck, cv, bias_c, bias_n)


def _spatial_kernel(u_ref, v_ref, ws_ref, bs_ref, o_ref, *, L):
    row = lax.broadcasted_iota(jnp.int32, (L, L), 0)
    col = lax.broadcasted_iota(jnp.int32, (L, L), 1)
    causal = row >= col
    gw = DC // G_C
    for g in range(G_C):
        cs = slice(g * gw, (g + 1) * gw)
        wm = jnp.where(causal, ws_ref[g], 0.0).astype(BF16)
        sv = jnp.dot(wm, v_ref[:, cs].astype(BF16), preferred_element_type=F32) + bs_ref[:, g:g + 1]
        o_ref[:, cs] = (u_ref[:, cs] * sv).astype(BF16)


def _spatial(u, v, ws, bs_t, *, L):
    m = u.shape[0]
    return pl.pallas_call(
        functools.partial(_spatial_kernel, L=L),
        grid=(m // L,),
        in_specs=[
            pl.BlockSpec((L, DC), lambda i: (i, 0)),
            pl.BlockSpec((L, DC), lambda i: (i, 0)),
            pl.BlockSpec(ws.shape, lambda i: (0, 0, 0)),
            pl.BlockSpec(bs_t.shape, lambda i: (0, 0)),
        ],
        out_specs=pl.BlockSpec((L, DC), lambda i: (i, 0)),
        out_shape=jax.ShapeDtypeStruct((m, DC), BF16),
        compiler_params=_cparams(1),
        name="spatial",
    )(u, v, ws, bs_t)


HALO = BF16_ROWS


def _ffn_up_kernel(xh_ref, x_ref, w_ref, hist_ref, cw_ref, cb_ref, h_ref, st_ref,
                   xb_ref, us_ref, *, tm, tiles_per_seq):
    i = pl.program_id(0)
    j = pl.program_id(1)
    tn = FF_TILE

    @pl.when(j == 0)
    def _():
        xb_ref[0:HALO, :] = xh_ref[...]
        xb_ref[HALO:, :] = x_ref[...]

    uz = jnp.dot(xb_ref[...], w_ref[0], preferred_element_type=F32)
    us_ref[...] = uz[:, :tn]
    z = uz[HALO:, tn:]

    @pl.when(i % tiles_per_seq == 0)
    def _():
        us_ref[HALO - SUBLANE:HALO, :] = hist_ref[0]

    cv = (us_ref[HALO - 2:HALO - 2 + tm, :] * cw_ref[0:1, :]
          + us_ref[HALO - 1:HALO - 1 + tm, :] * cw_ref[1:2, :]
          + us_ref[HALO:HALO + tm, :] * cw_ref[2:3, :])
    h_ref[...] = (jax.nn.gelu(cv + cb_ref[...]) * z).astype(BF16)
    st_ref[0] = us_ref[HALO + tm - SUBLANE:HALO + tm, :]


def _ffn_up_fused(xb, w12, hist8, cw, cb, *, B, T, tm):
    m = xb.shape[0]
    tps = T // tm
    tn = FF_TILE
    halo_blocks_per_tile = tm // HALO
    return pl.pallas_call(
        functools.partial(_ffn_up_kernel, tm=tm, tiles_per_seq=tps),
        grid=(m // tm, N_FF_TILES),
        in_specs=[
            pl.BlockSpec((HALO, D_MODEL), lambda i, j: (jnp.maximum(i * halo_blocks_per_tile - 1, 0), 0)),
            pl.BlockSpec((tm, D_MODEL), lambda i, j: (i, 0)),
            pl.BlockSpec((1, D_MODEL, 2 * tn), lambda i, j: (j, 0, 0)),
            pl.BlockSpec((1, SUBLANE, tn), lambda i, j: (i // tps, 0, j)),
            pl.BlockSpec((CONV_W, tn), lambda i, j: (0, j)),
            pl.BlockSpec((1, tn), lambda i, j: (0, j)),
        ],
        out_specs=[
            pl.BlockSpec((tm, tn), lambda i, j: (i, j)),
            pl.BlockSpec((1, SUBLANE, tn), lambda i, j: (i, 0, j)),
        ],
        out_shape=[
            jax.ShapeDtypeStruct((m, D_FF_PAD), BF16),
            jax.ShapeDtypeStruct((m // tm, SUBLANE, D_FF_PAD), F32),
        ],
        scratch_shapes=[pltpu.VMEM((HALO + tm, D_MODEL), BF16),
                        pltpu.VMEM((HALO + tm, tn), F32)],
        compiler_params=_cparams(2),
        name="ffn_up",
    )(xb, xb, w12, hist8, cw, cb)


def _conv_gate_kernel(u_ref, z_ref, hist_ref, cw_ref, cb_ref, h_ref, *, T):
    ue = jnp.concatenate([hist_ref[0], u_ref[...]], axis=0)
    cv = (ue[SUBLANE - 2:SUBLANE - 2 + T, :] * cw_ref[0:1, :]
          + ue[SUBLANE - 1:SUBLANE - 1 + T, :] * cw_ref[1:2, :]
          + ue[SUBLANE:SUBLANE + T, :] * cw_ref[2:3, :])
    h_ref[...] = (jax.nn.gelu(cv + cb_ref[...]) * z_ref[...]).astype(BF16)


def _conv_gate(uz, hist8, cw, cb, *, B, T):
    tn = FF_TILE
    return pl.pallas_call(
        functools.partial(_conv_gate_kernel, T=T),
        grid=(B, N_FF_TILES),
        in_specs=[
            pl.BlockSpec((T, tn), lambda b, j: (b, 2 * j)),
            pl.BlockSpec((T, tn), lambda b, j: (b, 2 * j + 1)),
            pl.BlockSpec((1, SUBLANE, tn), lambda b, j: (b, 0, j)),
            pl.BlockSpec((CONV_W, tn), lambda b, j: (0, j)),
            pl.BlockSpec((1, tn), lambda b, j: (0, j)),
        ],
        out_specs=pl.BlockSpec((T, tn), lambda b, j: (b, j)),
        out_shape=jax.ShapeDtypeStruct((B * T, D_FF_PAD), BF16),
        compiler_params=_cparams(2),
        name="conv_gate",
    )(uz, uz, hist8, cw, cb)


def _ffn_down_kernel(h_ref, w_ref, res_ref, g_ref, b_ref, y_ref, yb_ref, acc_ref, *, nk):
    k = pl.program_id(1)

    @pl.when(k == 0)
    def _():
        acc_ref[...] = jnp.zeros_like(acc_ref)

    acc_ref[...] += jnp.dot(h_ref[...], w_ref[...], preferred_element_type=F32)

    @pl.when(k == nk - 1)
    def _():
        y = _layer_norm(ALPHA * res_ref[...] + acc_ref[...], g_ref[...], b_ref[...])
        y_ref[...] = y
        yb_ref[...] = y.astype(BF16)


def _ffn_down(h, w3, res, g, b, *, tm, tk=FF_TILE):
    m = h.shape[0]
    nk = D_FF_PAD // tk
    n = D_MODEL
    return pl.pallas_call(
        functools.partial(_ffn_down_kernel, nk=nk),
        grid=(m // tm, nk),
        in_specs=[
            pl.BlockSpec((tm, tk), lambda i, k: (i, k)),
            pl.BlockSpec((tk, n), lambda i, k: (k, 0)),
            pl.BlockSpec((tm, n), lambda i, k: (i, 0)),
            pl.BlockSpec((1, n), lambda i, k: (0, 0)),
            pl.BlockSpec((1, n), lambda i, k: (0, 0)),
        ],
        out_specs=[pl.BlockSpec((tm, n), lambda i, k: (i, 0))] * 2,
        out_shape=[jax.ShapeDtypeStruct((m, n), F32), jax.ShapeDtypeStruct((m, n), BF16)],
        scratch_shapes=[pltpu.VMEM((tm, n), F32)],
        compiler_params=_cparams(2),
        name="ffn_down",
    )(h, w3, res, g.reshape(1, n), b.reshape(1, n))


def _prep_even(w_in, w_gate_up, b_gate, w_out):
    hk = H_A * DK_A
    hv = H_A * DV_A
    hd = H_B * DH_B
    o = 0
    w_qk = w_in[:, o:o + 2 * hk]; o += 2 * hk
    w_v = w_in[:, o:o + hv]; o += hv
    w_g = w_in[:, o:o + GATE_RANK]; o += GATE_RANK
    w_og = w_in[:, o:o + hv]; o += hv
    w_qb = w_in[:, o:o + hd]; o += hd
    w_kv = w_in[:, o:o + 2 * hd]
    w_qkg = jnp.concatenate([w_qk, jnp.pad(w_g, ((0, 0), (0, GATE_PAD - GATE_RANK)))], axis=1)
    wup = jnp.pad(w_gate_up, ((0, GATE_PAD - GATE_RANK), (0, 0)))
    return dict(
        w_qkg=w_qkg.astype(BF16), w_v=w_v.astype(BF16), w_og=w_og.astype(BF16),
        w_qb=w_qb.astype(BF16), w_kv=w_kv.astype(BF16),
        wup=wup.astype(BF16), bg=b_gate.reshape(1, hk),
        w_out_a=w_out[:hv].astype(BF16), w_out_b=w_out[hv:].astype(BF16),
    )


def _prep_ffn(w1, w2, cw, cb, w3):
    pad = D_FF_PAD - D_FF
    w1p = jnp.pad(w1, ((0, 0), (0, pad))).reshape(D_MODEL, N_FF_TILES, FF_TILE)
    w2p = jnp.pad(w2, ((0, 0), (0, pad))).reshape(D_MODEL, N_FF_TILES, FF_TILE)
    w12 = jnp.concatenate([w1p, w2p], axis=2).transpose(1, 0, 2).astype(BF16)
    return dict(
        w12=w12,
        cw=jnp.pad(cw, ((0, 0), (0, pad))),
        cb=jnp.pad(cb, (0, pad)).reshape(1, D_FF_PAD),
        w3=jnp.pad(w3, ((0, pad), (0, 0))).astype(BF16),
    )


def _rel_bias_table(table, rel):
    return table[:, jnp.clip(rel, -REL_CLIP, REL_CLIP) + REL_CLIP].astype(F32)


def _prompt_bias(table):
    r = jnp.arange(CHUNK)[:, None]
    j = jnp.arange(BAND)[None, :]
    out = []
    for cp in range(N_PREV_CHUNKS + 1):
        bias = _rel_bias_table(table, cp * CHUNK + r - j)
        valid = j < (cp + 1) * CHUNK
        out.append(jnp.where(valid[None], bias, NEG_INF))
    return jnp.stack(out)


def _step_bias(table, T, lc):
    qpos = PAST_LEN + jnp.arange(T)
    kpos = jnp.concatenate([PAST_LEN - lc + jnp.arange(lc), qpos])
    bias = _rel_bias_table(table, qpos[:, None] - kpos[None, :])
    return bias[:, :, :lc], bias[:, :, lc:]


def _conv_ffn(x, xb, F, ln_g, ln_b, hist, *, B, T, tm):
    pad = D_FF_PAD - D_FF
    hist8 = jnp.pad(hist, ((0, 0), (SUBLANE - (CONV_W - 1), 0), (0, pad)))
    if T >= tm:
        h, st8 = _ffn_up_fused(xb, F["w12"], hist8, F["cw"], F["cb"], B=B, T=T, tm=tm)
        new_state = st8.reshape(B, T // tm, SUBLANE, D_FF_PAD)[:, -1, SUBLANE - (CONV_W - 1):, :D_FF]
    else:
        m = B * T
        w12_flat = F["w12"].transpose(1, 0, 2).reshape(D_MODEL, 2 * D_FF_PAD)
        uz = _linear(xb, w12_flat, out_dtype=F32, tm=m, tn=2 * FF_TILE)
        h = _conv_gate(uz, hist8, F["cw"], F["cb"], B=B, T=T)
        u = uz.reshape(B, T, N_FF_TILES, 2, FF_TILE)[:, T - (CONV_W - 1):, :, 0, :]
        new_state = u.reshape(B, CONV_W - 1, D_FF_PAD)[:, :, :D_FF]
    y, yb = _ffn_down(h, F["w3"], x, ln_g, ln_b, tm=min(tm, B * T))
    return y, yb, new_state


def _trunk(x3, E, Fs, O, norms, state, bias):
    B, T, _ = x3.shape
    m = B * T
    tm = min(512, m)
    x = x3.reshape(m, D_MODEL)
    hk, hv, hd = H_A * DK_A, H_A * DV_A, H_B * DH_B
    prompt = state is None

    qkg, xb = _linear(x, E["w_qkg"], out_dtype=F32, tm=tm, tn=QKG_W, emit_xb=True)
    v_a = _linear(xb, E["w_v"], out_dtype=BF16, tm=tm, tn=hv)
    og = _linear(xb, E["w_og"], out_dtype=F32, tm=tm, tn=hv)
    q_b = _linear(xb, E["w_qb"], out_dtype=BF16, tm=tm, tn=hd)
    if prompt:
        s0_t = jnp.zeros((B, H_A, DV_A, DK_A), F32)
        conv_hist = [jnp.zeros((B, CONV_W - 1, D_FF), F32)] * DEPTH
    else:
        cache_k, cache_v, state_gla, state_conv = state
        s0_t = jnp.swapaxes(state_gla[0], -1, -2)
        conv_hist = [state_conv[i] for i in range(DEPTH)]
    o_a, s_fin_t = _gla(qkg, v_a, og, E["wup"], E["bg"], s0_t, E["gn"], B=B, T=T)
    new_gla = jnp.swapaxes(s_fin_t, -1, -2)[None]
    if prompt:
        kv_b = _linear(xb, E["w_kv"], out_dtype=BF16, tm=tm, tn=hd)
        o_b = _band_attn(q_b, kv_b, bias, B=B, T=T)
        keep = min(BAND_PAST, T)
        blocks_per_seq = T // keep
        kv_tail = _linear(xb, E["w_kv"], out_dtype=F32, tm=keep, tn=hd, n_row_blocks=B,
                          x_block_index=lambda i: i * blocks_per_seq + blocks_per_seq - 1)
        kv_rows = kv_tail.reshape(B, keep, 2, H_B, DH_B)
    else:
        kv_f = _linear(xb, E["w_kv"], out_dtype=F32, tm=tm, tn=hd)
        lc = cache_k.shape[2]
        o_b = _attn_step(q_b, kv_f, cache_k[0].reshape(B, lc, hd), cache_v[0].reshape(B, lc, hd),
                         bias[0], bias[1], B=B, T=T)
        kv_rows = kv_f.reshape(B, T, 2, H_B, DH_B)
    new_k = kv_rows[:, :, 0][None]
    new_v = kv_rows[:, :, 1][None]
    x, xb = _proj_ln([o_a, o_b], [E["w_out_a"], E["w_out_b"]], x, norms["ln1_g"][0], norms["ln1_b"][0], tm=tm)
    x, xb, conv0 = _conv_ffn(x, xb, Fs[0], norms["ln2_g"][0], norms["ln2_b"][0], conv_hist[0], B=B, T=T, tm=tm)

    u = _linear(xb, O["w_u"], out_dtype=F32, tm=tm, tn=DC // 2, act="gelu")
    (v,) = _proj_ln([xb], [O["w_v"]], None, O["ln_v_g"], O["ln_v_b"], tm=tm, pre_gelu=True, emit_bf16=False)
    L = min(T, CHUNK_C)
    gated = _spatial(u, v, O["ws"][:, :L, :L], jnp.transpose(O["bs"][:, :L]), L=L)
    x, xb = _proj_ln([gated], [O["w_out"]], x, norms["ln1_g"][1], norms["ln1_b"][1], tm=tm)
    x, xb, conv1 = _conv_ffn(x, xb, Fs[1], norms["ln2_g"][1], norms["ln2_b"][1], conv_hist[1], B=B, T=T, tm=tm)

    y = x.reshape(B, T, D_MODEL)
    new_conv = jnp.stack([conv0, conv1])
    mlp_v = v.reshape(B, T, DC)[None]
    return y, new_k, new_v, new_gla, new_conv, mlp_v


def kernel(x_prompt, x_sample, cache_attn_k, cache_attn_v, state_gla, state_ffn_conv, w_in_even, w_gate_up, b_gate, gla_norm_g, rel_bias, w_out_even, w_in_odd, ln_v_g, ln_v_b, w_spatial, b_spatial, w_out_odd, ffn_w1, ffn_w2, ffn_conv_w, ffn_conv_b, ffn_w3, ln1_g, ln1_b, ln2_g, ln2_b):
    E = _prep_even(w_in_even[0], w_gate_up[0], b_gate[0], w_out_even[0])
    E["gn"] = gla_norm_g[0]
    Fs = [_prep_ffn(ffn_w1[i], ffn_w2[i], ffn_conv_w[i], ffn_conv_b[i], ffn_w3[i]) for i in range(DEPTH)]
    O = dict(w_u=w_in_odd[0][:, :DC].astype(BF16), w_v=w_in_odd[0][:, DC:].astype(BF16),
             ln_v_g=ln_v_g[0], ln_v_b=ln_v_b[0], ws=w_spatial[0], bs=b_spatial[0],
             w_out=w_out_odd[0].astype(BF16))
    norms = dict(ln1_g=ln1_g, ln1_b=ln1_b, ln2_g=ln2_g, ln2_b=ln2_b)

    bias_p = _prompt_bias(rel_bias[0])
    y_p, k_p, v_p, gla_p, conv_p, _ = _trunk(x_prompt, E, Fs, O, norms, None, bias_p)

    bias_s = _step_bias(rel_bias[0], x_sample.shape[1], cache_attn_k.shape[2])
    y_s, k_s, v_s, gla_s, conv_s, mlp_v_s = _trunk(
        x_sample, E, Fs, O, norms, (cache_attn_k, cache_attn_v, state_gla, state_ffn_conv), bias_s)
    return (y_p, y_s, k_p, v_p, gla_p, conv_p, k_s, v_s, gla_s, conv_s, mlp_v_s)
```

```python
import functools

import jax
import jax.numpy as jnp
from jax import lax
from jax.experimental import pallas as pl
from jax.experimental.pallas import tpu as pltpu

F32 = jnp.float32
BF16 = jnp.bfloat16

D_MODEL = 2048
DEPTH = 2
CHUNK = 64
H_A = 4
DK_A = D_MODEL // 16
DV_A = D_MODEL // 8
GATE_RANK = 16
GATE_TEMP = 16.0
H_B = 8
DH_B = D_MODEL // 16
N_PREV_CHUNKS = 8
BAND_PAST = N_PREV_CHUNKS * CHUNK
REL_CLIP = 128
CHUNK_C = 128
DC = D_MODEL
G_C = 8
D_FF = ((8 * D_MODEL // 3 + 127) // 128) * 128
CONV_W = 3
ALPHA = (2 * DEPTH) ** 0.25
LN_EPS = 1e-5
NEG_INF = -1e30
PAST_LEN = 2048

LANE = 128
SUBLANE = 8
MXU_N = 256
FF_TILE = 512
D_FF_PAD = ((D_FF + FF_TILE - 1) // FF_TILE) * FF_TILE
N_FF_TILES = D_FF_PAD // FF_TILE
GATE_PAD = LANE
QKG_W = 2 * H_A * DK_A + GATE_PAD
ATT_QB = 4 * CHUNK
ATT_KW = BAND_PAST + ATT_QB
ROW_SUB = 128
VMEM_LIMIT = 56 * 1024 * 1024


def _cparams(n_axes, vmem=VMEM_LIMIT, **kw):
    return pltpu.CompilerParams(dimension_semantics=("arbitrary",) * n_axes,
                                vmem_limit_bytes=vmem, **kw)


def _resident(shape):
    nd = len(shape)
    return pl.BlockSpec(shape, lambda *_: (0,) * nd, pipeline_mode=pl.Buffered(1))


def _layer_norm(xf, g, b):
    mu = jnp.mean(xf, -1, keepdims=True)
    xc = xf - mu
    var = jnp.mean(xc * xc, -1, keepdims=True)
    y = xc * lax.rsqrt(var + LN_EPS)
    return y * g + b


def _log_sigmoid(x):
    return jnp.minimum(x, 0.0) - jnp.log1p(jnp.exp(-jnp.abs(x)))


def _linear_kernel(x_ref, w_ref, o_ref, *maybe_xb_ref):
    xb = x_ref[...].astype(BF16)
    o_ref[...] = jnp.dot(xb, w_ref[...], preferred_element_type=F32).astype(o_ref.dtype)
    if maybe_xb_ref:
        maybe_xb_ref[0][...] = xb


def _linear(x, w, *, out_dtype, tm, emit_xb=False, x_block_index=None, n_row_blocks=None):
    m, k = x.shape
    n = w.shape[1]
    n_rows = (m // tm) if n_row_blocks is None else n_row_blocks
    xmap = (lambda i: (i, 0)) if x_block_index is None else (lambda i: (x_block_index(i), 0))
    out_shape = [jax.ShapeDtypeStruct((n_rows * tm, n), out_dtype)]
    out_specs = [pl.BlockSpec((tm, n), lambda i: (i, 0))]
    if emit_xb:
        out_shape.append(jax.ShapeDtypeStruct((m, k), BF16))
        out_specs.append(pl.BlockSpec((tm, k), lambda i: (i, 0)))
    res = pl.pallas_call(
        _linear_kernel,
        grid=(n_rows,),
        in_specs=[pl.BlockSpec((tm, k), xmap), _resident(w.shape)],
        out_specs=out_specs,
        out_shape=out_shape,
        compiler_params=_cparams(1),
        name="linear",
    )(x, w)
    return res if emit_xb else res[0]


def _proj_ln_kernel(xa_ref, xb_ref, wa_ref, wb_ref, res_ref, g_ref, b_ref, y_ref, yb_ref, *, tm):
    for r in range(tm // ROW_SUB):
        rows = slice(r * ROW_SUB, (r + 1) * ROW_SUB)
        h = (jnp.dot(xa_ref[rows, :], wa_ref[...], preferred_element_type=F32)
             + jnp.dot(xb_ref[rows, :], wb_ref[...], preferred_element_type=F32))
        y = _layer_norm(ALPHA * res_ref[rows, :] + h, g_ref[...], b_ref[...])
        y_ref[rows, :] = y
        yb_ref[rows, :] = y.astype(BF16)


def _proj_ln(xa, xb, wa, wb, res, g, b, *, tm):
    m, n = res.shape
    row = lambda width: pl.BlockSpec((tm, width), lambda i: (i, 0))
    vec = pl.BlockSpec((1, n), lambda i: (0, 0))
    return pl.pallas_call(
        functools.partial(_proj_ln_kernel, tm=tm),
        grid=(m // tm,),
        in_specs=[row(xa.shape[1]), row(xb.shape[1]), _resident(wa.shape), _resident(wb.shape),
                  row(n), vec, vec],
        out_specs=[row(n), row(n)],
        out_shape=[jax.ShapeDtypeStruct((m, n), F32), jax.ShapeDtypeStruct((m, n), BF16)],
        compiler_params=_cparams(1),
        name="proj_ln",
    )(xa, xb, wa, wb, res, g.reshape(1, n), b.reshape(1, n))


def _gla_kernel(qkg_ref, v_ref, og_ref, wup_ref, bg_ref, s0_ref, gn_ref,
                o_ref, sfin_ref, st_ref, *, L, n_chunks, bb):
    c = pl.program_id(1)

    @pl.when(c == 0)
    def _():
        st_ref[...] = s0_ref[...]

    hk = H_A * DK_A
    row = lax.broadcasted_iota(jnp.int32, (L, L), 0)
    col = lax.broadcasted_iota(jnp.int32, (L, L), 1)
    causal = row >= col
    tri = jnp.where(causal, 1.0, 0.0).astype(BF16)
    nt = (((1,), (1,)), ((), ()))
    tn_dims = (((0,), (0,)), ((), ()))

    for b in range(bb):
        glow = qkg_ref[b, :, 2 * hk:2 * hk + GATE_PAD].astype(BF16)
        gate = jnp.dot(glow, wup_ref[...], preferred_element_type=F32) + bg_ref[...]
        logg = _log_sigmoid(gate) * (1.0 / GATE_TEMP)
        hi = logg.astype(BF16)
        r1 = logg - hi.astype(F32)
        mid = r1.astype(BF16)
        lo = (r1 - mid.astype(F32)).astype(BF16)
        bcum = (jnp.dot(tri, hi, preferred_element_type=F32)
                + jnp.dot(tri, mid, preferred_element_type=F32)
                + jnp.dot(tri, lo, preferred_element_type=F32))
        for h in range(H_A):
            ks = slice(h * DK_A, (h + 1) * DK_A)
            vs = slice(h * DV_A, (h + 1) * DV_A)
            bh = bcum[:, ks]
            qh = qkg_ref[b, :, h * DK_A:(h + 1) * DK_A]
            kh = qkg_ref[b, :, hk + h * DK_A:hk + (h + 1) * DK_A]
            q_dec = ((qh * (DK_A ** -0.5)) * jnp.exp(bh)).astype(BF16)
            k_dec = (kh * jnp.exp(-bh)).astype(BF16)
            a = lax.dot_general(q_dec, k_dec, nt, preferred_element_type=F32)
            a = jnp.where(causal, a, 0.0).astype(BF16)
            vh = v_ref[b, :, vs]
            st = st_ref[b, h]
            o = (lax.dot_general(q_dec, st.astype(BF16), nt, preferred_element_type=F32)
                 + jnp.dot(a, vh, preferred_element_type=F32))
            b_last = bh[L - 1:L, :]
            k_upd = (kh * jnp.exp(b_last - bh)).astype(BF16)
            st_ref[b, h] = st * jnp.exp(b_last) + lax.dot_general(
                vh, k_upd, tn_dims, preferred_element_type=F32)
            on = o * lax.rsqrt(jnp.mean(o * o, -1, keepdims=True) + LN_EPS) * gn_ref[h:h + 1, :]
            o_ref[b, :, vs] = (on * jax.nn.silu(og_ref[b, :, vs])).astype(BF16)

    @pl.when(c == n_chunks - 1)
    def _():
        sfin_ref[...] = st_ref[...]


def _gla(qkg, v_a, og, wup, bg, s0_t, gn, *, B, T):
    L = min(T, CHUNK)
    n = T // L
    bb = min(B, 4)
    hv = H_A * DV_A
    seq = lambda width: pl.BlockSpec((bb, L, width), lambda g, c: (g, c, 0))
    const2 = lambda g, c: (0, 0)
    state = pl.BlockSpec((bb, H_A, DV_A, DK_A), lambda g, c: (g, 0, 0, 0))
    o, s_fin = pl.pallas_call(
        functools.partial(_gla_kernel, L=L, n_chunks=n, bb=bb),
        grid=(B // bb, n),
        in_specs=[seq(QKG_W), seq(hv), seq(hv),
                  pl.BlockSpec(wup.shape, const2), pl.BlockSpec(bg.shape, const2),
                  state, pl.BlockSpec(gn.shape, const2)],
        out_specs=[seq(hv), state],
        out_shape=[jax.ShapeDtypeStruct((B, T, hv), BF16),
                   jax.ShapeDtypeStruct((B, H_A, DV_A, DK_A), F32)],
        scratch_shapes=[pltpu.VMEM((bb, H_A, DV_A, DK_A), F32)],
        compiler_params=_cparams(2),
        name="gla",
    )(qkg.reshape(B, T, QKG_W), v_a.reshape(B, T, hv), og.reshape(B, T, hv), wup, bg, s0_t, gn)
    return o.reshape(B * T, hv), s_fin


def _band_attn_kernel(q_ref, kv_ref, bias_ref, o_ref):
    qb = pl.program_id(1)
    start = pl.multiple_of(jnp.maximum(qb * ATT_QB - BAND_PAST, 0), ATT_QB)
    hd = H_B * DH_B
    nt = (((1,), (1,)), ((), ()))
    for h in range(H_B):
        hs = slice(h * DH_B, (h + 1) * DH_B)
        q = q_ref[:, hs]
        k = kv_ref[pl.ds(start, ATT_KW), h * DH_B:(h + 1) * DH_B]
        v = kv_ref[pl.ds(start, ATT_KW), hd + h * DH_B:hd + (h + 1) * DH_B]
        s = lax.dot_general(q, k, nt, preferred_element_type=F32) * (DH_B ** -0.5) + bias_ref[0, h]
        m = jnp.max(s, -1, keepdims=True)
        e = jnp.exp(s - m)
        p = e / jnp.sum(e, -1, keepdims=True)
        o_ref[:, hs] = jnp.dot(p.astype(BF16), v, preferred_element_type=F32).astype(BF16)


def _band_attn(q_b, kv_b, bias_blocks, *, B, T):
    assert T % ATT_QB == 0 and T >= ATT_KW
    n = T // ATT_QB
    hd = H_B * DH_B
    last_bias = bias_blocks.shape[0] - 1
    return pl.pallas_call(
        _band_attn_kernel,
        grid=(B, n),
        in_specs=[
            pl.BlockSpec((ATT_QB, hd), lambda b, c: (b * n + c, 0)),
            pl.BlockSpec((T, 2 * hd), lambda b, c: (b, 0), pipeline_mode=pl.Buffered(1)),
            pl.BlockSpec((1, H_B, ATT_QB, ATT_KW), lambda b, c: (jnp.minimum(c, last_bias), 0, 0, 0)),
        ],
        out_specs=pl.BlockSpec((ATT_QB, hd), lambda b, c: (b * n + c, 0)),
        out_shape=jax.ShapeDtypeStruct((B * T, hd), BF16),
        compiler_params=_cparams(2),
        name="band_attn",
    )(q_b, kv_b, bias_blocks)


def _attn_step_kernel(q_ref, kv_ref, ck_ref, cv_ref, bias_c_ref, bias_n_ref, o_ref):
    hd = H_B * DH_B
    nt = (((1,), (1,)), ((), ()))
    scale = DH_B ** -0.5
    for h in range(H_B):
        hs = slice(h * DH_B, (h + 1) * DH_B)
        q = q_ref[:, hs].astype(BF16)
        kc = ck_ref[0, :, hs].astype(BF16)
        vc = cv_ref[0, :, hs].astype(BF16)
        kn = kv_ref[:, hs].astype(BF16)
        vn = kv_ref[:, hd + h * DH_B:hd + (h + 1) * DH_B].astype(BF16)
        sc = lax.dot_general(q, kc, nt, preferred_element_type=F32) * scale + bias_c_ref[h]
        sn = lax.dot_general(q, kn, nt, preferred_element_type=F32) * scale + bias_n_ref[h]
        m = jnp.maximum(jnp.max(sc, -1, keepdims=True), jnp.max(sn, -1, keepdims=True))
        ec = jnp.exp(sc - m)
        en = jnp.exp(sn - m)
        den = jnp.sum(ec, -1, keepdims=True) + jnp.sum(en, -1, keepdims=True)
        o = (jnp.dot((ec / den).astype(BF16), vc, preferred_element_type=F32)
             + jnp.dot((en / den).astype(BF16), vn, preferred_element_type=F32))
        o_ref[:, hs] = o.astype(BF16)


def _attn_step(q_b, kv_b, ck, cv, bias_c, bias_n, *, B, T):
    hd = H_B * DH_B
    lc = ck.shape[1]
    return pl.pallas_call(
        _attn_step_kernel,
        grid=(B,),
        in_specs=[
            pl.BlockSpec((T, hd), lambda b: (b, 0)),
            pl.BlockSpec((T, 2 * hd), lambda b: (b, 0)),
            pl.BlockSpec((1, lc, hd), lambda b: (b, 0, 0)),
            pl.BlockSpec((1, lc, hd), lambda b: (b, 0, 0)),
            pl.BlockSpec(bias_c.shape, lambda b: (0, 0, 0)),
            pl.BlockSpec(bias_n.shape, lambda b: (0, 0, 0)),
        ],
        out_specs=pl.BlockSpec((T, hd), lambda b: (b, 0)),
        out_shape=jax.ShapeDtypeStruct((B * T, hd), BF16),
        compiler_params=_cparams(1),
        name="attn_step",
    )(q_b, kv_b, ck, cv, bias_c, bias_n)


def _gmlp_kernel(xb_ref, x_ref, win_ref, lvg_ref, lvb_ref, ws_ref, bs_ref, wout_ref, g_ref, b_ref,
                 y_ref, yb_ref, v_ref, *, tm, L):
    row = lax.broadcasted_iota(jnp.int32, (L, L), 0)
    col = lax.broadcasted_iota(jnp.int32, (L, L), 1)
    causal = row >= col
    gw = DC // G_C
    wms = [jnp.where(causal, ws_ref[g], 0.0).astype(BF16) for g in range(G_C)]
    sub = max(ROW_SUB, L)
    for r in range(tm // sub):
        rows = slice(r * sub, (r + 1) * sub)
        uv = jax.nn.gelu(jnp.dot(xb_ref[rows, :], win_ref[...], preferred_element_type=F32))
        u = uv[:, :DC]
        v = _layer_norm(uv[:, DC:], lvg_ref[...], lvb_ref[...])
        v_ref[rows, :] = v
        vb = v.astype(BF16)
        chunks = []
        for c in range(sub // L):
            cr = slice(c * L, (c + 1) * L)
            groups = []
            for g in range(G_C):
                cs = slice(g * gw, (g + 1) * gw)
                sv = jnp.dot(wms[g], vb[cr, cs], preferred_element_type=F32) + bs_ref[:, g:g + 1]
                groups.append((u[cr, cs] * sv).astype(BF16))
            chunks.append(jnp.concatenate(groups, axis=1))
        gated = chunks[0] if len(chunks) == 1 else jnp.concatenate(chunks, axis=0)
        h = jnp.dot(gated, wout_ref[...], preferred_element_type=F32)
        y = _layer_norm(ALPHA * x_ref[rows, :] + h, g_ref[...], b_ref[...])
        y_ref[rows, :] = y
        yb_ref[rows, :] = y.astype(BF16)


def _gmlp(xb, x, O, g, b, *, T, tm):
    m = x.shape[0]
    L = min(T, CHUNK_C)
    ws = O["ws"][:, :L, :L]
    bs_t = jnp.transpose(O["bs"][:, :L])
    row = pl.BlockSpec((tm, D_MODEL), lambda i: (i, 0))
    vec = pl.BlockSpec((1, D_MODEL), lambda i: (0, 0))
    return pl.pallas_call(
        functools.partial(_gmlp_kernel, tm=tm, L=L),
        grid=(m // tm,),
        in_specs=[row, row, _resident(O["w_in"].shape), vec, vec,
                  pl.BlockSpec(ws.shape, lambda i: (0, 0, 0)), pl.BlockSpec(bs_t.shape, lambda i: (0, 0)),
                  _resident(O["w_out"].shape), vec, vec],
        out_specs=[row, row, row],
        out_shape=[jax.ShapeDtypeStruct((m, D_MODEL), F32), jax.ShapeDtypeStruct((m, D_MODEL), BF16),
                   jax.ShapeDtypeStruct((m, DC), F32)],
        compiler_params=_cparams(1),
        name="gmlp",
    )(xb, x, O["w_in"], O["ln_v_g"].reshape(1, DC), O["ln_v_b"].reshape(1, DC), ws, bs_t,
      O["w_out"], g.reshape(1, D_MODEL), b.reshape(1, D_MODEL))


def _ffn_up_kernel(x_ref, w1_ref, w2_ref, hist_ref, cw_ref, cb_ref, h_ref, st_ref,
                   wb_ref, carry_ref, *, tm, seg, tiles_per_seq):
    j = pl.program_id(0)
    i = pl.program_id(1)
    tn = FF_TILE

    @pl.when(i == 0)
    def _():
        keep = j * tn + lax.broadcasted_iota(jnp.int32, (1, tn), 1) < D_FF
        wb_ref[:, :tn] = jnp.where(keep, w1_ref[...], 0.0).astype(BF16)
        wb_ref[:, tn:] = jnp.where(keep, w2_ref[...], 0.0).astype(BF16)
        carry_ref[...] = jnp.zeros_like(carry_ref)

    n_seg = tm // seg
    rg = min(tm, 256)
    piece = min(seg, rg)
    seq_start = i % tiles_per_seq == 0
    sub_i = lax.broadcasted_iota(jnp.int32, (SUBLANE, MXU_N), 0)
    for c in range(tn // MXU_N):
        cs = slice(c * MXU_N, (c + 1) * MXU_N)
        prev = None
        for r in range(tm // rg):
            x = x_ref[r * rg:(r + 1) * rg, :]
            u = jnp.dot(x, wb_ref[:, c * MXU_N:(c + 1) * MXU_N], preferred_element_type=F32)
            z = jnp.dot(x, wb_ref[:, tn + c * MXU_N:tn + (c + 1) * MXU_N], preferred_element_type=F32)
            for p in range(rg // piece):
                row0 = r * rg + p * piece
                s = row0 // seg
                if row0 % seg == 0:
                    if n_seg == 1:
                        prev = jnp.where(seq_start, hist_ref[0, :, cs], carry_ref[:, cs])
                    else:
                        prev = hist_ref[s, :, cs]
                up = u[p * piece:(p + 1) * piece]
                zp = z[p * piece:(p + 1) * piece]
                u1 = pltpu.roll(up, 1, 0)
                u2 = pltpu.roll(up, 2, 0)
                h1 = jnp.where(sub_i < 1, pltpu.roll(prev, 1, 0), u1[:SUBLANE])
                h2 = jnp.where(sub_i < 2, pltpu.roll(prev, 2, 0), u2[:SUBLANE])
                u1 = jnp.concatenate([h1, u1[SUBLANE:]], axis=0)
                u2 = jnp.concatenate([h2, u2[SUBLANE:]], axis=0)
                cv = u2 * cw_ref[0:1, cs] + u1 * cw_ref[1:2, cs] + up * cw_ref[2:3, cs]
                h_ref[row0:row0 + piece, cs] = (jax.nn.gelu(cv + cb_ref[:, cs]) * zp).astype(BF16)
                prev = up[piece - SUBLANE:]
                if (row0 + piece) % seg == 0:
                    st_ref[s, :, cs] = prev
                    if n_seg == 1:
                        carry_ref[:, cs] = prev


def _ffn_up(xb, w1, w2, hist8, cw, cb, *, B, T, tm):
    m = xb.shape[0]
    tn = FF_TILE
    seg = min(T, tm)
    n_seg = tm // seg
    tps = T // seg
    wspec = pl.BlockSpec((D_MODEL, tn), lambda j, i: (0, j))
    return pl.pallas_call(
        functools.partial(_ffn_up_kernel, tm=tm, seg=seg, tiles_per_seq=tps),
        grid=(N_FF_TILES, m // tm),
        in_specs=[
            pl.BlockSpec((tm, D_MODEL), lambda j, i: (i, 0)),
            wspec, wspec,
            pl.BlockSpec((n_seg, SUBLANE, tn), lambda j, i: (i // tps, 0, j)),
            pl.BlockSpec((CONV_W, tn), lambda j, i: (0, j)),
            pl.BlockSpec((1, tn), lambda j, i: (0, j)),
        ],
        out_specs=[
            pl.BlockSpec((tm, tn), lambda j, i: (i, j)),
            pl.BlockSpec((n_seg, SUBLANE, tn), lambda j, i: (i, 0, j)),
        ],
        out_shape=[
            jax.ShapeDtypeStruct((m, D_FF_PAD), BF16),
            jax.ShapeDtypeStruct((m // seg, SUBLANE, D_FF_PAD), F32),
        ],
        scratch_shapes=[pltpu.VMEM((D_MODEL, 2 * tn), BF16), pltpu.VMEM((SUBLANE, tn), F32)],
        compiler_params=_cparams(2),
        name="ffn_up",
    )(xb, w1, w2, hist8, cw, cb)


def _ffn_down_kernel(h_ref, w_ref, res_ref, g_ref, b_ref, y_ref, yb_ref, *, tm):
    for r in range(tm // ROW_SUB):
        rows = slice(r * ROW_SUB, (r + 1) * ROW_SUB)
        acc = jnp.dot(h_ref[rows, :], w_ref[...], preferred_element_type=F32)
        y = _layer_norm(ALPHA * res_ref[rows, :] + acc, g_ref[...], b_ref[...])
        y_ref[rows, :] = y
        yb_ref[rows, :] = y.astype(BF16)


def _ffn_down(h, w3, res, g, b, *, tm):
    m, n = res.shape
    row = lambda width: pl.BlockSpec((tm, width), lambda i: (i, 0))
    vec = pl.BlockSpec((1, n), lambda i: (0, 0))
    return pl.pallas_call(
        functools.partial(_ffn_down_kernel, tm=tm),
        grid=(m // tm,),
        in_specs=[row(D_FF_PAD), _resident(w3.shape), row(n), vec, vec],
        out_specs=[row(n), row(n)],
        out_shape=[jax.ShapeDtypeStruct((m, n), F32), jax.ShapeDtypeStruct((m, n), BF16)],
        compiler_params=_cparams(1),
        name="ffn_down",
    )(h, w3, res, g.reshape(1, n), b.reshape(1, n))


def _cast_pad_rows_kernel(w_ref, o_ref, *, rows_valid, tr):
    k = pl.program_id(0)
    keep = k * tr + lax.broadcasted_iota(jnp.int32, (tr, 1), 0) < rows_valid
    o_ref[...] = jnp.where(keep, w_ref[...], 0.0).astype(BF16)


def _cast_pad_rows(w, rows_pad, *, tr=FF_TILE):
    rows, n = w.shape
    return pl.pallas_call(
        functools.partial(_cast_pad_rows_kernel, rows_valid=rows, tr=tr),
        grid=(rows_pad // tr,),
        in_specs=[pl.BlockSpec((tr, n), lambda k: (k, 0))],
        out_specs=pl.BlockSpec((tr, n), lambda k: (k, 0)),
        out_shape=jax.ShapeDtypeStruct((rows_pad, n), BF16),
        compiler_params=_cparams(1),
        name="cast_pad_rows",
    )(w)


def _prep_even(w_in, w_gate_up, b_gate, w_out):
    hk = H_A * DK_A
    hv = H_A * DV_A
    hd = H_B * DH_B
    o = 0
    w_qk = w_in[:, o:o + 2 * hk]; o += 2 * hk
    w_v = w_in[:, o:o + hv]; o += hv
    w_g = w_in[:, o:o + GATE_RANK]; o += GATE_RANK
    w_og = w_in[:, o:o + hv]; o += hv
    w_qb = w_in[:, o:o + hd]; o += hd
    w_kv = w_in[:, o:o + 2 * hd]
    w_qkg = jnp.concatenate([w_qk, jnp.pad(w_g, ((0, 0), (0, GATE_PAD - GATE_RANK)))], axis=1)
    wup = jnp.pad(w_gate_up, ((0, GATE_PAD - GATE_RANK), (0, 0)))
    return dict(
        w_qkg=w_qkg.astype(BF16), w_v=w_v.astype(BF16), w_og=w_og.astype(BF16),
        w_qb=w_qb.astype(BF16), w_kv=w_kv.astype(BF16),
        wup=wup.astype(BF16), bg=b_gate.reshape(1, hk),
        w_out_a=w_out[:hv].astype(BF16), w_out_b=w_out[hv:].astype(BF16),
    )


def _prep_ffn(w1, w2, cw, cb, w3):
    pad = D_FF_PAD - D_FF
    return dict(w1=w1, w2=w2,
                cw=jnp.pad(cw, ((0, 0), (0, pad))),
                cb=jnp.pad(cb, (0, pad)).reshape(1, D_FF_PAD),
                w3=_cast_pad_rows(w3, D_FF_PAD))


def _rel_toeplitz(table, n_rows, n_cols, offset):
    heads = table.shape[0]
    d_min = offset - (n_cols - 1)
    d_max = offset + (n_rows - 1)
    assert d_min <= -REL_CLIP and d_max >= REL_CLIP
    ext = jnp.concatenate([
        jnp.broadcast_to(table[:, :1], (heads, -REL_CLIP - d_min)), table,
        jnp.broadcast_to(table[:, -1:], (heads, d_max - REL_CLIP))], axis=1)
    length = ext.shape[1]
    rev = jnp.pad(ext[:, ::-1], ((0, 0), (0, 1)))
    skew = jnp.tile(rev, (1, n_rows))[:, :n_rows * length].reshape(heads, n_rows, length)
    return skew[:, :, n_rows - 1:n_rows - 1 + n_cols].astype(F32)


def _prompt_bias(table):
    n_special = BAND_PAST // ATT_QB
    toe = _rel_toeplitz(table, ATT_QB, ATT_KW + BAND_PAST, BAND_PAST)
    r = jnp.arange(ATT_QB)[:, None]
    j = jnp.arange(ATT_KW)[None, :]
    out = []
    for qb in range(n_special + 1):
        q_chunk = (qb * ATT_QB + r) // CHUNK
        valid = (j >= q_chunk * CHUNK - BAND_PAST) & (j < (q_chunk + 1) * CHUNK)
        shift = (n_special - qb) * ATT_QB
        out.append(jnp.where(valid[None], toe[:, :, shift:shift + ATT_KW], NEG_INF))
    return jnp.stack(out)


def _step_bias(table, T, lc):
    span = max(T, REL_CLIP + 1)
    bias = _rel_toeplitz(table, span, lc + span, lc)[:, :T, :lc + T]
    return bias[:, :, :lc], bias[:, :, lc:]


def _conv_ffn(x, xb, F, ln_g, ln_b, hist, *, B, T, tm_up, tm_down):
    pad = D_FF_PAD - D_FF
    keep = CONV_W - 1
    hist8 = jnp.pad(hist, ((0, 0), (SUBLANE - keep, 0), (0, pad)))
    h, st8 = _ffn_up(xb, F["w1"], F["w2"], hist8, F["cw"], F["cb"], B=B, T=T, tm=tm_up)
    new_state = st8.reshape(B, -1, SUBLANE, D_FF_PAD)[:, -1, SUBLANE - keep:, :D_FF]
    y, yb = _ffn_down(h, F["w3"], x, ln_g, ln_b, tm=tm_down)
    return y, yb, new_state


def _trunk(x3, E, Fs, O, norms, state, bias):
    B, T, _ = x3.shape
    m = B * T
    tm = min(512, m)
    tm_up = min(1024, m)
    tm_small = min(256, m)
    x = x3.reshape(m, D_MODEL)
    hv, hd = H_A * DV_A, H_B * DH_B
    prompt = state is None

    qkg, xb = _linear(x, E["w_qkg"], out_dtype=F32, tm=tm, emit_xb=True)
    v_a = _linear(xb, E["w_v"], out_dtype=BF16, tm=tm)
    og = _linear(xb, E["w_og"], out_dtype=F32, tm=tm)
    q_b = _linear(xb, E["w_qb"], out_dtype=BF16, tm=tm)
    if prompt:
        s0_t = jnp.zeros((B, H_A, DV_A, DK_A), F32)
        conv_hist = [jnp.zeros((B, CONV_W - 1, D_FF), F32)] * DEPTH
    else:
        cache_k, cache_v, state_gla, state_conv = state
        s0_t = jnp.swapaxes(state_gla[0], -1, -2)
        conv_hist = [state_conv[i] for i in range(DEPTH)]
    o_a, s_fin_t = _gla(qkg, v_a, og, E["wup"], E["bg"], s0_t, E["gn"], B=B, T=T)
    new_gla = jnp.swapaxes(s_fin_t, -1, -2)[None]
    if prompt:
        kv_b = _linear(xb, E["w_kv"], out_dtype=BF16, tm=tm)
        o_b = _band_attn(q_b, kv_b, bias, B=B, T=T)
        keep = min(BAND_PAST, T)
        blocks_per_seq = T // keep
        kv_tail = _linear(xb, E["w_kv"], out_dtype=F32, tm=keep, n_row_blocks=B,
                          x_block_index=lambda i: i * blocks_per_seq + blocks_per_seq - 1)
        kv_rows = kv_tail.reshape(B, keep, 2, H_B, DH_B)
    else:
        kv_f = _linear(xb, E["w_kv"], out_dtype=F32, tm=tm)
        lc = cache_k.shape[2]
        o_b = _attn_step(q_b, kv_f, cache_k[0].reshape(B, lc, hd), cache_v[0].reshape(B, lc, hd),
                         bias[0], bias[1], B=B, T=T)
        kv_rows = kv_f.reshape(B, T, 2, H_B, DH_B)
    new_k = kv_rows[:, :, 0][None]
    new_v = kv_rows[:, :, 1][None]
    x, xb = _proj_ln(o_a, o_b, E["w_out_a"], E["w_out_b"], x, norms["ln1_g"][0], norms["ln1_b"][0], tm=tm)
    x, xb, conv0 = _conv_ffn(x, xb, Fs[0], norms["ln2_g"][0], norms["ln2_b"][0], conv_hist[0],
                             B=B, T=T, tm_up=tm_up, tm_down=tm_small)

    x, xb, v = _gmlp(xb, x, O, norms["ln1_g"][1], norms["ln1_b"][1], T=T, tm=tm_small)
    x, xb, conv1 = _conv_ffn(x, xb, Fs[1], norms["ln2_g"][1], norms["ln2_b"][1], conv_hist[1],
                             B=B, T=T, tm_up=tm_up, tm_down=tm_small)

    y = x.reshape(B, T, D_MODEL)
    new_conv = jnp.stack([conv0, conv1])
    mlp_v = v.reshape(B, T, DC)[None]
    return y, new_k, new_v, new_gla, new_conv, mlp_v


def kernel(x_prompt, x_sample, cache_attn_k, cache_attn_v, state_gla, state_ffn_conv, w_in_even, w_gate_up, b_gate, gla_norm_g, rel_bias, w_out_even, w_in_odd, ln_v_g, ln_v_b, w_spatial, b_spatial, w_out_odd, ffn_w1, ffn_w2, ffn_conv_w, ffn_conv_b, ffn_w3, ln1_g, ln1_b, ln2_g, ln2_b):
    E = _prep_even(w_in_even[0], w_gate_up[0], b_gate[0], w_out_even[0])
    E["gn"] = gla_norm_g[0]
    Fs = [_prep_ffn(ffn_w1[i], ffn_w2[i], ffn_conv_w[i], ffn_conv_b[i], ffn_w3[i]) for i in range(DEPTH)]
    O = dict(w_in=w_in_odd[0].astype(BF16), ln_v_g=ln_v_g[0], ln_v_b=ln_v_b[0],
             ws=w_spatial[0], bs=b_spatial[0], w_out=w_out_odd[0].astype(BF16))
    norms = dict(ln1_g=ln1_g, ln1_b=ln1_b, ln2_g=ln2_g, ln2_b=ln2_b)

    bias_p = _prompt_bias(rel_bias[0])
    y_p, k_p, v_p, gla_p, conv_p, _ = _trunk(x_prompt, E, Fs, O, norms, None, bias_p)

    bias_s = _step_bias(rel_bias[0], x_sample.shape[1], cache_attn_k.shape[2])
    y_s, k_s, v_s, gla_s, conv_s, mlp_v_s = _trunk(
        x_sample, E, Fs, O, norms, (cache_attn_k, cache_attn_v, state_gla, state_ffn_conv), bias_s)
    return (y_p, y_s, k_p, v_p, gla_p, conv_p, k_s, v_s, gla_s, conv_s, mlp_v_s)
```

```python
import functools

import jax
import jax.numpy as jnp
from jax import lax
from jax.experimental import pallas as pl
from jax.experimental.pallas import tpu as pltpu

F32 = jnp.float32
BF16 = jnp.bfloat16

D_MODEL = 2048
DEPTH = 2
CHUNK = 64
H_A = 4
DK_A = D_MODEL // 16
DV_A = D_MODEL // 8
GATE_RANK = 16
GATE_TEMP = 16.0
H_B = 8
DH_B = D_MODEL // 16
N_PREV_CHUNKS = 8
BAND_PAST = N_PREV_CHUNKS * CHUNK
REL_CLIP = 128
CHUNK_C = 128
DC = D_MODEL
G_C = 8
D_FF = ((8 * D_MODEL // 3 + 127) // 128) * 128
CONV_W = 3
ALPHA = (2 * DEPTH) ** 0.25
LN_EPS = 1e-5
NEG_INF = -1e30
PAST_LEN = 2048

LANE = 128
SUBLANE = 8
MXU_N = 256
FF_TILE = 512
D_FF_PAD = ((D_FF + FF_TILE - 1) // FF_TILE) * FF_TILE
N_FF_TILES = D_FF_PAD // FF_TILE
GATE_PAD = LANE
QKG_W = 2 * H_A * DK_A + GATE_PAD
ATT_QB = 4 * CHUNK
ATT_KW = BAND_PAST + ATT_QB
ROW_SUB = 128
VMEM_LIMIT = 56 * 1024 * 1024


def _cparams(n_axes, vmem=VMEM_LIMIT, **kw):
    return pltpu.CompilerParams(dimension_semantics=("arbitrary",) * n_axes,
                                vmem_limit_bytes=vmem, **kw)


def _resident(shape):
    nd = len(shape)
    return pl.BlockSpec(shape, lambda *_: (0,) * nd, pipeline_mode=pl.Buffered(1))


def _layer_norm(xf, g, b):
    mu = jnp.mean(xf, -1, keepdims=True)
    xc = xf - mu
    var = jnp.mean(xc * xc, -1, keepdims=True)
    y = xc * lax.rsqrt(var + LN_EPS)
    return y * g + b


def _log_sigmoid(x):
    return jnp.minimum(x, 0.0) - jnp.log1p(jnp.exp(-jnp.abs(x)))


def _linear_kernel(x_ref, w_ref, o_ref, *maybe_xb_ref):
    xb = x_ref[...].astype(BF16)
    o_ref[...] = jnp.dot(xb, w_ref[...], preferred_element_type=F32).astype(o_ref.dtype)
    if maybe_xb_ref:
        maybe_xb_ref[0][...] = xb


def _linear(x, w, *, out_dtype, tm, emit_xb=False, x_block_index=None, n_row_blocks=None):
    m, k = x.shape
    n = w.shape[1]
    n_rows = (m // tm) if n_row_blocks is None else n_row_blocks
    xmap = (lambda i: (i, 0)) if x_block_index is None else (lambda i: (x_block_index(i), 0))
    out_shape = [jax.ShapeDtypeStruct((n_rows * tm, n), out_dtype)]
    out_specs = [pl.BlockSpec((tm, n), lambda i: (i, 0))]
    if emit_xb:
        out_shape.append(jax.ShapeDtypeStruct((m, k), BF16))
        out_specs.append(pl.BlockSpec((tm, k), lambda i: (i, 0)))
    res = pl.pallas_call(
        _linear_kernel,
        grid=(n_rows,),
        in_specs=[pl.BlockSpec((tm, k), xmap), _resident(w.shape)],
        out_specs=out_specs,
        out_shape=out_shape,
        compiler_params=_cparams(1),
        name="linear",
    )(x, w)
    return res if emit_xb else res[0]


def _proj_ln_kernel(xa_ref, xb_ref, wa_ref, wb_ref, res_ref, g_ref, b_ref, y_ref, yb_ref, *, tm):
    for r in range(tm // ROW_SUB):
        rows = slice(r * ROW_SUB, (r + 1) * ROW_SUB)
        h = (jnp.dot(xa_ref[rows, :], wa_ref[...], preferred_element_type=F32)
             + jnp.dot(xb_ref[rows, :], wb_ref[...], preferred_element_type=F32))
        y = _layer_norm(ALPHA * res_ref[rows, :] + h, g_ref[...], b_ref[...])
        y_ref[rows, :] = y
        yb_ref[rows, :] = y.astype(BF16)


def _proj_ln(xa, xb, wa, wb, res, g, b, *, tm):
    m, n = res.shape
    row = lambda width: pl.BlockSpec((tm, width), lambda i: (i, 0))
    vec = pl.BlockSpec((1, n), lambda i: (0, 0))
    return pl.pallas_call(
        functools.partial(_proj_ln_kernel, tm=tm),
        grid=(m // tm,),
        in_specs=[row(xa.shape[1]), row(xb.shape[1]), _resident(wa.shape), _resident(wb.shape),
                  row(n), vec, vec],
        out_specs=[row(n), row(n)],
        out_shape=[jax.ShapeDtypeStruct((m, n), F32), jax.ShapeDtypeStruct((m, n), BF16)],
        compiler_params=_cparams(1),
        name="proj_ln",
    )(xa, xb, wa, wb, res, g.reshape(1, n), b.reshape(1, n))


def _gla_kernel(qkg_ref, v_ref, og_ref, wup_ref, bg_ref, s0_ref, gn_ref,
                o_ref, sfin_ref, st_ref, *, L, n_chunks, bb):
    c = pl.program_id(1)

    @pl.when(c == 0)
    def _():
        st_ref[...] = s0_ref[...]

    hk = H_A * DK_A
    row = lax.broadcasted_iota(jnp.int32, (L, L), 0)
    col = lax.broadcasted_iota(jnp.int32, (L, L), 1)
    causal = row >= col
    tri = jnp.where(causal, 1.0, 0.0).astype(BF16)
    nt = (((1,), (1,)), ((), ()))
    tn_dims = (((0,), (0,)), ((), ()))

    for b in range(bb):
        glow = qkg_ref[b, :, 2 * hk:2 * hk + GATE_PAD].astype(BF16)
        gate = jnp.dot(glow, wup_ref[...], preferred_element_type=F32) + bg_ref[...]
        logg = _log_sigmoid(gate) * (1.0 / GATE_TEMP)
        hi = logg.astype(BF16)
        r1 = logg - hi.astype(F32)
        mid = r1.astype(BF16)
        lo = (r1 - mid.astype(F32)).astype(BF16)
        bcum = (jnp.dot(tri, hi, preferred_element_type=F32)
                + jnp.dot(tri, mid, preferred_element_type=F32)
                + jnp.dot(tri, lo, preferred_element_type=F32))
        for h in range(H_A):
            ks = slice(h * DK_A, (h + 1) * DK_A)
            vs = slice(h * DV_A, (h + 1) * DV_A)
            bh = bcum[:, ks]
            qh = qkg_ref[b, :, h * DK_A:(h + 1) * DK_A]
            kh = qkg_ref[b, :, hk + h * DK_A:hk + (h + 1) * DK_A]
            q_dec = ((qh * (DK_A ** -0.5)) * jnp.exp(bh)).astype(BF16)
            k_dec = (kh * jnp.exp(-bh)).astype(BF16)
            a = lax.dot_general(q_dec, k_dec, nt, preferred_element_type=F32)
            a = jnp.where(causal, a, 0.0).astype(BF16)
            vh = v_ref[b, :, vs]
            st = st_ref[b, h]
            o = (lax.dot_general(q_dec, st.astype(BF16), nt, preferred_element_type=F32)
                 + jnp.dot(a, vh, preferred_element_type=F32))
            b_last = bh[L - 1:L, :]
            k_upd = (kh * jnp.exp(b_last - bh)).astype(BF16)
            st_ref[b, h] = st * jnp.exp(b_last) + lax.dot_general(
                vh, k_upd, tn_dims, preferred_element_type=F32)
            on = o * lax.rsqrt(jnp.mean(o * o, -1, keepdims=True) + LN_EPS) * gn_ref[h:h + 1, :]
            o_ref[b, :, vs] = (on * jax.nn.silu(og_ref[b, :, vs])).astype(BF16)

    @pl.when(c == n_chunks - 1)
    def _():
        sfin_ref[...] = st_ref[...]


def _gla(qkg, v_a, og, wup, bg, s0_t, gn, *, B, T):
    L = min(T, CHUNK)
    n = T // L
    bb = min(B, 4)
    hv = H_A * DV_A
    seq = lambda width: pl.BlockSpec((bb, L, width), lambda g, c: (g, c, 0))
    const2 = lambda g, c: (0, 0)
    state = pl.BlockSpec((bb, H_A, DV_A, DK_A), lambda g, c: (g, 0, 0, 0))
    o, s_fin = pl.pallas_call(
        functools.partial(_gla_kernel, L=L, n_chunks=n, bb=bb),
        grid=(B // bb, n),
        in_specs=[seq(QKG_W), seq(hv), seq(hv),
                  pl.BlockSpec(wup.shape, const2), pl.BlockSpec(bg.shape, const2),
                  state, pl.BlockSpec(gn.shape, const2)],
        out_specs=[seq(hv), state],
        out_shape=[jax.ShapeDtypeStruct((B, T, hv), BF16),
                   jax.ShapeDtypeStruct((B, H_A, DV_A, DK_A), F32)],
        scratch_shapes=[pltpu.VMEM((bb, H_A, DV_A, DK_A), F32)],
        compiler_params=_cparams(2),
        name="gla",
    )(qkg.reshape(B, T, QKG_W), v_a.reshape(B, T, hv), og.reshape(B, T, hv), wup, bg, s0_t, gn)
    return o.reshape(B * T, hv), s_fin


def _band_attn_kernel(q_ref, kv_ref, bias_ref, o_ref):
    qb = pl.program_id(1)
    start = pl.multiple_of(jnp.maximum(qb * ATT_QB - BAND_PAST, 0), ATT_QB)
    hd = H_B * DH_B
    nt = (((1,), (1,)), ((), ()))
    for h in range(H_B):
        hs = slice(h * DH_B, (h + 1) * DH_B)
        q = q_ref[:, hs]
        k = kv_ref[pl.ds(start, ATT_KW), h * DH_B:(h + 1) * DH_B]
        v = kv_ref[pl.ds(start, ATT_KW), hd + h * DH_B:hd + (h + 1) * DH_B]
        s = lax.dot_general(q, k, nt, preferred_element_type=F32) * (DH_B ** -0.5) + bias_ref[0, h]
        m = jnp.max(s, -1, keepdims=True)
        e = jnp.exp(s - m)
        p = e / jnp.sum(e, -1, keepdims=True)
        o_ref[:, hs] = jnp.dot(p.astype(BF16), v, preferred_element_type=F32).astype(BF16)


def _band_attn(q_b, kv_b, bias_blocks, *, B, T):
    assert T % ATT_QB == 0 and T >= ATT_KW
    n = T // ATT_QB
    hd = H_B * DH_B
    last_bias = bias_blocks.shape[0] - 1
    return pl.pallas_call(
        _band_attn_kernel,
        grid=(B, n),
        in_specs=[
            pl.BlockSpec((ATT_QB, hd), lambda b, c: (b * n + c, 0)),
            pl.BlockSpec((T, 2 * hd), lambda b, c: (b, 0), pipeline_mode=pl.Buffered(1)),
            pl.BlockSpec((1, H_B, ATT_QB, ATT_KW), lambda b, c: (jnp.minimum(c, last_bias), 0, 0, 0)),
        ],
        out_specs=pl.BlockSpec((ATT_QB, hd), lambda b, c: (b * n + c, 0)),
        out_shape=jax.ShapeDtypeStruct((B * T, hd), BF16),
        compiler_params=_cparams(2),
        name="band_attn",
    )(q_b, kv_b, bias_blocks)


def _attn_step_kernel(q_ref, kv_ref, ck_ref, cv_ref, bias_c_ref, bias_n_ref, o_ref):
    hd = H_B * DH_B
    lc = ck_ref.shape[0] // H_B
    nt = (((1,), (1,)), ((), ()))
    scale = DH_B ** -0.5
    for h in range(H_B):
        hs = slice(h * DH_B, (h + 1) * DH_B)
        q = q_ref[:, hs].astype(BF16)
        kc = ck_ref[pl.ds(h, lc, stride=H_B), :].astype(BF16)
        vc = cv_ref[pl.ds(h, lc, stride=H_B), :].astype(BF16)
        kn = kv_ref[:, hs].astype(BF16)
        vn = kv_ref[:, hd + h * DH_B:hd + (h + 1) * DH_B].astype(BF16)
        sc = lax.dot_general(q, kc, nt, preferred_element_type=F32) * scale + bias_c_ref[h]
        sn = lax.dot_general(q, kn, nt, preferred_element_type=F32) * scale + bias_n_ref[h]
        m = jnp.maximum(jnp.max(sc, -1, keepdims=True), jnp.max(sn, -1, keepdims=True))
        ec = jnp.exp(sc - m)
        en = jnp.exp(sn - m)
        den = jnp.sum(ec, -1, keepdims=True) + jnp.sum(en, -1, keepdims=True)
        o = (jnp.dot((ec / den).astype(BF16), vc, preferred_element_type=F32)
             + jnp.dot((en / den).astype(BF16), vn, preferred_element_type=F32))
        o_ref[:, hs] = o.astype(BF16)


def _attn_step(q_b, kv_b, cache_k, cache_v, bias_c, bias_n, *, B, T):
    hd = H_B * DH_B
    lc = cache_k.shape[2]
    ck = cache_k.reshape(1, B, lc * H_B, DH_B)
    cv = cache_v.reshape(1, B, lc * H_B, DH_B)
    cache = pl.BlockSpec((None, None, lc * H_B, DH_B), lambda b: (0, b, 0, 0))
    return pl.pallas_call(
        _attn_step_kernel,
        grid=(B,),
        in_specs=[
            pl.BlockSpec((T, hd), lambda b: (b, 0)),
            pl.BlockSpec((T, 2 * hd), lambda b: (b, 0)),
            cache, cache,
            pl.BlockSpec(bias_c.shape, lambda b: (0, 0, 0)),
            pl.BlockSpec(bias_n.shape, lambda b: (0, 0, 0)),
        ],
        out_specs=pl.BlockSpec((T, hd), lambda b: (b, 0)),
        out_shape=jax.ShapeDtypeStruct((B * T, hd), BF16),
        compiler_params=_cparams(1),
        name="attn_step",
    )(q_b, kv_b, ck, cv, bias_c, bias_n)


def _gmlp_kernel(xb_ref, x_ref, win_ref, lvg_ref, lvb_ref, ws_ref, bs_ref, wout_ref, g_ref, b_ref,
                 y_ref, yb_ref, v_ref, *, tm, L):
    row = lax.broadcasted_iota(jnp.int32, (L, L), 0)
    col = lax.broadcasted_iota(jnp.int32, (L, L), 1)
    causal = row >= col
    gw = DC // G_C
    wms = [jnp.where(causal, ws_ref[g], 0.0).astype(BF16) for g in range(G_C)]
    sub = max(ROW_SUB, L)
    for r in range(tm // sub):
        rows = slice(r * sub, (r + 1) * sub)
        uv = jax.nn.gelu(jnp.dot(xb_ref[rows, :], win_ref[...], preferred_element_type=F32))
        u = uv[:, :DC]
        v = _layer_norm(uv[:, DC:], lvg_ref[...], lvb_ref[...])
        v_ref[rows, :] = v
        vb = v.astype(BF16)
        chunks = []
        for c in range(sub // L):
            cr = slice(c * L, (c + 1) * L)
            groups = []
            for g in range(G_C):
                cs = slice(g * gw, (g + 1) * gw)
                sv = jnp.dot(wms[g], vb[cr, cs], preferred_element_type=F32) + bs_ref[:, g:g + 1]
                groups.append((u[cr, cs] * sv).astype(BF16))
            chunks.append(jnp.concatenate(groups, axis=1))
        gated = chunks[0] if len(chunks) == 1 else jnp.concatenate(chunks, axis=0)
        h = jnp.dot(gated, wout_ref[...], preferred_element_type=F32)
        y = _layer_norm(ALPHA * x_ref[rows, :] + h, g_ref[...], b_ref[...])
        y_ref[rows, :] = y
        yb_ref[rows, :] = y.astype(BF16)


def _gmlp(xb, x, O, g, b, *, T, tm):
    m = x.shape[0]
    L = min(T, CHUNK_C)
    ws = O["ws"][:, :L, :L]
    bs_t = jnp.transpose(O["bs"][:, :L])
    row = pl.BlockSpec((tm, D_MODEL), lambda i: (i, 0))
    vec = pl.BlockSpec((1, D_MODEL), lambda i: (0, 0))
    return pl.pallas_call(
        functools.partial(_gmlp_kernel, tm=tm, L=L),
        grid=(m // tm,),
        in_specs=[row, row, _resident(O["w_in"].shape), vec, vec,
                  pl.BlockSpec(ws.shape, lambda i: (0, 0, 0)), pl.BlockSpec(bs_t.shape, lambda i: (0, 0)),
                  _resident(O["w_out"].shape), vec, vec],
        out_specs=[row, row, row],
        out_shape=[jax.ShapeDtypeStruct((m, D_MODEL), F32), jax.ShapeDtypeStruct((m, D_MODEL), BF16),
                   jax.ShapeDtypeStruct((m, DC), F32)],
        compiler_params=_cparams(1),
        name="gmlp",
    )(xb, x, O["w_in"], O["ln_v_g"].reshape(1, DC), O["ln_v_b"].reshape(1, DC), ws, bs_t,
      O["w_out"], g.reshape(1, D_MODEL), b.reshape(1, D_MODEL))


def _ffn_up_kernel(x_ref, w1_ref, w2_ref, hist_ref, cw_ref, cb_ref, h_ref, st_ref,
                   wb_ref, carry_ref, *, tm, seg, tiles_per_seq):
    j = pl.program_id(0)
    i = pl.program_id(1)
    tn = FF_TILE

    @pl.when(i == 0)
    def _():
        keep = j * tn + lax.broadcasted_iota(jnp.int32, (1, tn), 1) < D_FF
        wb_ref[:, :tn] = jnp.where(keep, w1_ref[...], 0.0).astype(BF16)
        wb_ref[:, tn:] = jnp.where(keep, w2_ref[...], 0.0).astype(BF16)
        carry_ref[...] = jnp.zeros_like(carry_ref)

    n_seg = tm // seg
    rg = min(tm, 256)
    piece = min(seg, rg)
    seq_start = i % tiles_per_seq == 0
    sub_i = lax.broadcasted_iota(jnp.int32, (SUBLANE, MXU_N), 0)
    for c in range(tn // MXU_N):
        cs = slice(c * MXU_N, (c + 1) * MXU_N)
        prev = None
        for r in range(tm // rg):
            x = x_ref[r * rg:(r + 1) * rg, :]
            u = jnp.dot(x, wb_ref[:, c * MXU_N:(c + 1) * MXU_N], preferred_element_type=F32)
            z = jnp.dot(x, wb_ref[:, tn + c * MXU_N:tn + (c + 1) * MXU_N], preferred_element_type=F32)
            for p in range(rg // piece):
                row0 = r * rg + p * piece
                s = row0 // seg
                if row0 % seg == 0:
                    if n_seg == 1:
                        prev = jnp.where(seq_start, hist_ref[0, :, cs], carry_ref[:, cs])
                    else:
                        prev = hist_ref[s, :, cs]
                up = u[p * piece:(p + 1) * piece]
                zp = z[p * piece:(p + 1) * piece]
                u1 = pltpu.roll(up, 1, 0)
                u2 = pltpu.roll(up, 2, 0)
                h1 = jnp.where(sub_i < 1, pltpu.roll(prev, 1, 0), u1[:SUBLANE])
                h2 = jnp.where(sub_i < 2, pltpu.roll(prev, 2, 0), u2[:SUBLANE])
                u1 = jnp.concatenate([h1, u1[SUBLANE:]], axis=0)
                u2 = jnp.concatenate([h2, u2[SUBLANE:]], axis=0)
                cv = u2 * cw_ref[0:1, cs] + u1 * cw_ref[1:2, cs] + up * cw_ref[2:3, cs]
                h_ref[row0:row0 + piece, cs] = (jax.nn.gelu(cv + cb_ref[:, cs]) * zp).astype(BF16)
                prev = up[piece - SUBLANE:]
                if (row0 + piece) % seg == 0:
                    st_ref[s, :, cs] = prev
                    if n_seg == 1:
                        carry_ref[:, cs] = prev


def _ffn_up(xb, w1_all, w2_all, layer, hist8, cw, cb, *, T, tm):
    m = xb.shape[0]
    tn = FF_TILE
    seg = min(T, tm)
    n_seg = tm // seg
    tps = T // seg
    wspec = pl.BlockSpec((None, D_MODEL, tn), lambda j, i: (layer, 0, j))
    return pl.pallas_call(
        functools.partial(_ffn_up_kernel, tm=tm, seg=seg, tiles_per_seq=tps),
        grid=(N_FF_TILES, m // tm),
        in_specs=[
            pl.BlockSpec((tm, D_MODEL), lambda j, i: (i, 0)),
            wspec, wspec,
            pl.BlockSpec((n_seg, SUBLANE, tn), lambda j, i: (i // tps, 0, j)),
            pl.BlockSpec((CONV_W, tn), lambda j, i: (0, j)),
            pl.BlockSpec((1, tn), lambda j, i: (0, j)),
        ],
        out_specs=[
            pl.BlockSpec((tm, tn), lambda j, i: (i, j)),
            pl.BlockSpec((n_seg, SUBLANE, tn), lambda j, i: (i, 0, j)),
        ],
        out_shape=[
            jax.ShapeDtypeStruct((m, D_FF_PAD), BF16),
            jax.ShapeDtypeStruct((m // seg, SUBLANE, D_FF_PAD), F32),
        ],
        scratch_shapes=[pltpu.VMEM((D_MODEL, 2 * tn), BF16), pltpu.VMEM((SUBLANE, tn), F32)],
        compiler_params=_cparams(2),
        name="ffn_up",
    )(xb, w1_all, w2_all, hist8, cw, cb)


def _ffn_down_kernel(h_ref, w_ref, res_ref, g_ref, b_ref, y_ref, yb_ref, *, tm):
    for r in range(tm // ROW_SUB):
        rows = slice(r * ROW_SUB, (r + 1) * ROW_SUB)
        acc = jnp.dot(h_ref[rows, :], w_ref[...], preferred_element_type=F32)
        y = _layer_norm(ALPHA * res_ref[rows, :] + acc, g_ref[...], b_ref[...])
        y_ref[rows, :] = y
        yb_ref[rows, :] = y.astype(BF16)


def _ffn_down(h, w3, res, g, b, *, tm):
    m, n = res.shape
    row = lambda width: pl.BlockSpec((tm, width), lambda i: (i, 0))
    vec = pl.BlockSpec((1, n), lambda i: (0, 0))
    return pl.pallas_call(
        functools.partial(_ffn_down_kernel, tm=tm),
        grid=(m // tm,),
        in_specs=[row(D_FF_PAD), _resident(w3.shape), row(n), vec, vec],
        out_specs=[row(n), row(n)],
        out_shape=[jax.ShapeDtypeStruct((m, n), F32), jax.ShapeDtypeStruct((m, n), BF16)],
        compiler_params=_cparams(1),
        name="ffn_down",
    )(h, w3, res, g.reshape(1, n), b.reshape(1, n))


def _cast_pad_rows_kernel(w_ref, o_ref, *, rows_valid, tr):
    k = pl.program_id(0)
    keep = k * tr + lax.broadcasted_iota(jnp.int32, (tr, 1), 0) < rows_valid
    o_ref[...] = jnp.where(keep, w_ref[...], 0.0).astype(BF16)


def _cast_pad_rows(w_all, layer, rows_pad, *, tr=FF_TILE):
    _, rows, n = w_all.shape
    return pl.pallas_call(
        functools.partial(_cast_pad_rows_kernel, rows_valid=rows, tr=tr),
        grid=(rows_pad // tr,),
        in_specs=[pl.BlockSpec((None, tr, n), lambda k: (layer, k, 0))],
        out_specs=pl.BlockSpec((tr, n), lambda k: (k, 0)),
        out_shape=jax.ShapeDtypeStruct((rows_pad, n), BF16),
        compiler_params=_cparams(1),
        name="cast_pad_rows",
    )(w_all)


def _split_w_in_kernel(w_ref, qkg_ref, v_ref, og_ref, qb_ref, kv_ref):
    hk, hv, hd = H_A * DK_A, H_A * DV_A, H_B * DH_B
    qkg_ref[:, :2 * hk] = w_ref[:, :2 * hk].astype(BF16)
    o = 2 * hk
    v_ref[...] = w_ref[:, o:o + hv].astype(BF16)
    o += hv
    g = w_ref[:, o:o + GATE_PAD]
    lane = lax.broadcasted_iota(jnp.int32, g.shape, 1)
    qkg_ref[:, 2 * hk:] = jnp.where(lane < GATE_RANK, g, 0.0).astype(BF16)
    o += GATE_RANK
    og_ref[...] = w_ref[:, o:o + hv].astype(BF16)
    o += hv
    qb_ref[...] = w_ref[:, o:o + hd].astype(BF16)
    o += hd
    kv_ref[...] = w_ref[:, o:o + 2 * hd].astype(BF16)


def _split_w_in(w_in_all, *, tr=256):
    _, k, n = w_in_all.shape
    hv, hd = H_A * DV_A, H_B * DH_B
    widths = [QKG_W, hv, hv, hd, 2 * hd]
    return pl.pallas_call(
        _split_w_in_kernel,
        grid=(k // tr,),
        in_specs=[pl.BlockSpec((None, tr, n), lambda i: (0, i, 0))],
        out_specs=[pl.BlockSpec((tr, w), lambda i: (i, 0)) for w in widths],
        out_shape=[jax.ShapeDtypeStruct((k, w), BF16) for w in widths],
        compiler_params=_cparams(1),
        name="split_w_in",
    )(w_in_all)


def _prep_even(w_in_all, w_gate_up, b_gate, w_out):
    hk, hv = H_A * DK_A, H_A * DV_A
    w_qkg, w_v, w_og, w_qb, w_kv = _split_w_in(w_in_all)
    wup = jnp.pad(w_gate_up, ((0, GATE_PAD - GATE_RANK), (0, 0)))
    return dict(
        w_qkg=w_qkg, w_v=w_v, w_og=w_og, w_qb=w_qb, w_kv=w_kv,
        wup=wup.astype(BF16), bg=b_gate.reshape(1, hk),
        w_out_a=w_out[:hv].astype(BF16), w_out_b=w_out[hv:].astype(BF16),
    )


def _prep_ffn(cw, cb, w3_all, layer):
    pad = D_FF_PAD - D_FF
    return dict(layer=layer,
                cw=jnp.pad(cw, ((0, 0), (0, pad))),
                cb=jnp.pad(cb, (0, pad)).reshape(1, D_FF_PAD),
                w3=_cast_pad_rows(w3_all, layer, D_FF_PAD))


def _toeplitz_kernel(ext_ref, o_ref):
    heads, rows, width = o_ref.shape
    for h in range(heads):
        o_ref[h] = pltpu.roll(jnp.broadcast_to(ext_ref[h], (rows, width)), 0, 1, stride=1, stride_axis=0)


def _rel_toeplitz(table, n_rows, n_cols, offset):
    heads = table.shape[0]
    pad = -(-n_rows // LANE) * LANE
    width = -(-(pad + n_cols) // LANE) * LANE
    n_hi = offset + pad - REL_CLIP
    assert n_hi >= 0
    ext = jnp.concatenate([
        jnp.broadcast_to(table[:, -1:], (heads, n_hi)), table[:, ::-1],
        jnp.broadcast_to(table[:, :1], (heads, width))], axis=1)[:, :width]
    toe = pl.pallas_call(
        _toeplitz_kernel,
        grid=(1,),
        in_specs=[pl.BlockSpec((heads, 1, width), lambda i: (0, 0, 0))],
        out_specs=pl.BlockSpec((heads, n_rows, width), lambda i: (0, 0, 0)),
        out_shape=jax.ShapeDtypeStruct((heads, n_rows, width), F32),
        compiler_params=_cparams(1),
        name="rel_toeplitz",
    )(ext.astype(F32).reshape(heads, 1, width))
    return toe, pad


def _prompt_bias(table):
    n_special = BAND_PAST // ATT_QB
    toe, pad = _rel_toeplitz(table, ATT_QB, ATT_KW + BAND_PAST, BAND_PAST)
    r = jnp.arange(ATT_QB)[:, None]
    j = jnp.arange(ATT_KW)[None, :]
    out = []
    for qb in range(n_special + 1):
        q_chunk = (qb * ATT_QB + r) // CHUNK
        valid = (j >= q_chunk * CHUNK - BAND_PAST) & (j < (q_chunk + 1) * CHUNK)
        shift = pad + (n_special - qb) * ATT_QB
        out.append(jnp.where(valid[None], toe[:, :, shift:shift + ATT_KW], NEG_INF))
    return jnp.stack(out)


def _step_bias(table, T, lc):
    toe, pad = _rel_toeplitz(table, T, lc + T, lc)
    return toe[:, :, pad:pad + lc], toe[:, :, pad + lc:pad + lc + T]


def _conv_ffn(x, xb, F, ln_g, ln_b, hist, *, B, T, tm_up, tm_down):
    pad = D_FF_PAD - D_FF
    keep = CONV_W - 1
    hist8 = jnp.pad(hist, ((0, 0), (SUBLANE - keep, 0), (0, pad)))
    h, st8 = _ffn_up(xb, F["w1_all"], F["w2_all"], F["layer"], hist8, F["cw"], F["cb"], T=T, tm=tm_up)
    new_state = st8.reshape(B, -1, SUBLANE, D_FF_PAD)[:, -1, SUBLANE - keep:, :D_FF]
    y, yb = _ffn_down(h, F["w3"], x, ln_g, ln_b, tm=tm_down)
    return y, yb, new_state


def _trunk(x3, E, Fs, O, norms, state, bias):
    B, T, _ = x3.shape
    m = B * T
    tm = min(512, m)
    tm_up = min(2048, m)
    tm_small = min(256, m)
    x = x3.reshape(m, D_MODEL)
    hv, hd = H_A * DV_A, H_B * DH_B
    prompt = state is None

    qkg, xb = _linear(x, E["w_qkg"], out_dtype=F32, tm=tm, emit_xb=True)
    v_a = _linear(xb, E["w_v"], out_dtype=BF16, tm=tm)
    og = _linear(xb, E["w_og"], out_dtype=F32, tm=tm)
    q_b = _linear(xb, E["w_qb"], out_dtype=BF16, tm=tm)
    if prompt:
        s0_t = jnp.zeros((B, H_A, DV_A, DK_A), F32)
        conv_hist = [jnp.zeros((B, CONV_W - 1, D_FF), F32)] * DEPTH
    else:
        cache_k, cache_v, state_gla, state_conv = state
        s0_t = jnp.swapaxes(state_gla[0], -1, -2)
        conv_hist = [state_conv[i] for i in range(DEPTH)]
    o_a, s_fin_t = _gla(qkg, v_a, og, E["wup"], E["bg"], s0_t, E["gn"], B=B, T=T)
    new_gla = jnp.swapaxes(s_fin_t, -1, -2)[None]
    if prompt:
        kv_b = _linear(xb, E["w_kv"], out_dtype=BF16, tm=tm)
        o_b = _band_attn(q_b, kv_b, bias, B=B, T=T)
        keep = min(BAND_PAST, T)
        blocks_per_seq = T // keep
        kv_tail = _linear(xb, E["w_kv"], out_dtype=F32, tm=keep, n_row_blocks=B,
                          x_block_index=lambda i: i * blocks_per_seq + blocks_per_seq - 1)
        kv_rows = kv_tail.reshape(B, keep, 2, H_B, DH_B)
    else:
        kv_f = _linear(xb, E["w_kv"], out_dtype=F32, tm=tm)
        o_b = _attn_step(q_b, kv_f, cache_k, cache_v, bias[0], bias[1], B=B, T=T)
        kv_rows = kv_f.reshape(B, T, 2, H_B, DH_B)
    new_k = kv_rows[:, :, 0][None]
    new_v = kv_rows[:, :, 1][None]
    x, xb = _proj_ln(o_a, o_b, E["w_out_a"], E["w_out_b"], x, norms["ln1_g"][0], norms["ln1_b"][0], tm=tm)
    x, xb, conv0 = _conv_ffn(x, xb, Fs[0], norms["ln2_g"][0], norms["ln2_b"][0], conv_hist[0],
                             B=B, T=T, tm_up=tm_up, tm_down=tm_small)

    x, xb, v = _gmlp(xb, x, O, norms["ln1_g"][1], norms["ln1_b"][1], T=T, tm=tm_small)
    x, xb, conv1 = _conv_ffn(x, xb, Fs[1], norms["ln2_g"][1], norms["ln2_b"][1], conv_hist[1],
                             B=B, T=T, tm_up=tm_up, tm_down=tm_small)

    y = x.reshape(B, T, D_MODEL)
    new_conv = jnp.stack([conv0, conv1])
    mlp_v = v.reshape(B, T, DC)[None]
    return y, new_k, new_v, new_gla, new_conv, mlp_v


def kernel(x_prompt, x_sample, cache_attn_k, cache_attn_v, state_gla, state_ffn_conv, w_in_even, w_gate_up, b_gate, gla_norm_g, rel_bias, w_out_even, w_in_odd, ln_v_g, ln_v_b, w_spatial, b_spatial, w_out_odd, ffn_w1, ffn_w2, ffn_conv_w, ffn_conv_b, ffn_w3, ln1_g, ln1_b, ln2_g, ln2_b):
    E = _prep_even(w_in_even, w_gate_up[0], b_gate[0], w_out_even[0])
    E["gn"] = gla_norm_g[0]
    Fs = [dict(_prep_ffn(ffn_conv_w[i], ffn_conv_b[i], ffn_w3, i), w1_all=ffn_w1, w2_all=ffn_w2)
          for i in range(DEPTH)]
    O = dict(w_in=w_in_odd[0].astype(BF16), ln_v_g=ln_v_g[0], ln_v_b=ln_v_b[0],
             ws=w_spatial[0], bs=b_spatial[0], w_out=w_out_odd[0].astype(BF16))
    norms = dict(ln1_g=ln1_g, ln1_b=ln1_b, ln2_g=ln2_g, ln2_b=ln2_b)

    bias_p = _prompt_bias(rel_bias[0])
    y_p, k_p, v_p, gla_p, conv_p, _ = _trunk(x_prompt, E, Fs, O, norms, None, bias_p)

    bias_s = _step_bias(rel_bias[0], x_sample.shape[1], cache_attn_k.shape[2])
    y_s, k_s, v_s, gla_s, conv_s, mlp_v_s = _trunk(
        x_sample, E, Fs, O, norms, (cache_attn_k, cache_attn_v, state_gla, state_ffn_conv), bias_s)
    return (y_p, y_s, k_p, v_p, gla_p, conv_p, k_s, v_s, gla_s, conv_s, mlp_v_s)
```

```python
import functools

import jax
import jax.numpy as jnp
from jax import lax
from jax.experimental import pallas as pl
from jax.experimental.pallas import tpu as pltpu

F32 = jnp.float32
BF16 = jnp.bfloat16

D_MODEL = 2048
DEPTH = 2
CHUNK = 64
H_A = 4
DK_A = D_MODEL // 16
DV_A = D_MODEL // 8
GATE_RANK = 16
GATE_TEMP = 16.0
H_B = 8
DH_B = D_MODEL // 16
N_PREV_CHUNKS = 8
BAND_PAST = N_PREV_CHUNKS * CHUNK
REL_CLIP = 128
CHUNK_C = 128
DC = D_MODEL
G_C = 8
D_FF = ((8 * D_MODEL // 3 + 127) // 128) * 128
CONV_W = 3
ALPHA = (2 * DEPTH) ** 0.25
LN_EPS = 1e-5
NEG_INF = -1e30
PAST_LEN = 2048
LOG2E = 1.4426950408889634

LANE = 128
SUBLANE = 8
MXU_N = 256
FF_TILE = 512
D_FF_PAD = ((D_FF + FF_TILE - 1) // FF_TILE) * FF_TILE
N_FF_TILES = D_FF_PAD // FF_TILE
GATE_PAD = LANE
QKG_W = 2 * H_A * DK_A + GATE_PAD
ATT_QB = 4 * CHUNK
ATT_KW = BAND_PAST + ATT_QB
ROW_SUB = 128
VMEM_LIMIT = 56 * 1024 * 1024


def _cparams(n_axes, vmem=VMEM_LIMIT, **kw):
    return pltpu.CompilerParams(dimension_semantics=("arbitrary",) * n_axes,
                                vmem_limit_bytes=vmem, **kw)


def _resident(shape):
    nd = len(shape)
    return pl.BlockSpec(shape, lambda *_: (0,) * nd, pipeline_mode=pl.Buffered(1))


def _layer_norm(xf, g, b):
    mu = jnp.mean(xf, -1, keepdims=True)
    xc = xf - mu
    var = jnp.mean(xc * xc, -1, keepdims=True)
    y = xc * lax.rsqrt(var + LN_EPS)
    return y * g + b


def _log_sigmoid(x):
    return jnp.minimum(x, 0.0) - jnp.log1p(jnp.exp(-jnp.abs(x)))


def _linear_kernel(x_ref, w_ref, o_ref, *maybe_xb_ref):
    xb = x_ref[...].astype(BF16)
    o_ref[...] = jnp.dot(xb, w_ref[...], preferred_element_type=F32).astype(o_ref.dtype)
    if maybe_xb_ref:
        maybe_xb_ref[0][...] = xb


def _linear(x, w, *, out_dtype, tm, emit_xb=False, x_block_index=None, n_row_blocks=None):
    m, k = x.shape
    n = w.shape[1]
    n_rows = (m // tm) if n_row_blocks is None else n_row_blocks
    xmap = (lambda i: (i, 0)) if x_block_index is None else (lambda i: (x_block_index(i), 0))
    out_shape = [jax.ShapeDtypeStruct((n_rows * tm, n), out_dtype)]
    out_specs = [pl.BlockSpec((tm, n), lambda i: (i, 0))]
    if emit_xb:
        out_shape.append(jax.ShapeDtypeStruct((m, k), BF16))
        out_specs.append(pl.BlockSpec((tm, k), lambda i: (i, 0)))
    res = pl.pallas_call(
        _linear_kernel,
        grid=(n_rows,),
        in_specs=[pl.BlockSpec((tm, k), xmap), _resident(w.shape)],
        out_specs=out_specs,
        out_shape=out_shape,
        compiler_params=_cparams(1),
        name="linear",
    )(x, w)
    return res if emit_xb else res[0]


def _proj_ln_kernel(xa_ref, xb_ref, wa_ref, wb_ref, res_ref, g_ref, b_ref, y_ref, yb_ref, *, tm):
    for r in range(tm // ROW_SUB):
        rows = slice(r * ROW_SUB, (r + 1) * ROW_SUB)
        h = (jnp.dot(xa_ref[rows, :], wa_ref[...], preferred_element_type=F32)
             + jnp.dot(xb_ref[rows, :], wb_ref[...], preferred_element_type=F32))
        y = _layer_norm(ALPHA * res_ref[rows, :] + h, g_ref[...], b_ref[...])
        y_ref[rows, :] = y
        yb_ref[rows, :] = y.astype(BF16)


def _proj_ln(xa, xb, wa, wb, res, g, b, *, tm):
    m, n = res.shape
    row = lambda width: pl.BlockSpec((tm, width), lambda i: (i, 0))
    vec = pl.BlockSpec((1, n), lambda i: (0, 0))
    return pl.pallas_call(
        functools.partial(_proj_ln_kernel, tm=tm),
        grid=(m // tm,),
        in_specs=[row(xa.shape[1]), row(xb.shape[1]), _resident(wa.shape), _resident(wb.shape),
                  row(n), vec, vec],
        out_specs=[row(n), row(n)],
        out_shape=[jax.ShapeDtypeStruct((m, n), F32), jax.ShapeDtypeStruct((m, n), BF16)],
        compiler_params=_cparams(1),
        name="proj_ln",
    )(xa, xb, wa, wb, res, g.reshape(1, n), b.reshape(1, n))


def _gla_kernel(qkg_ref, v_ref, og_ref, wup_ref, bg_ref, s0_ref, gn_ref,
                o_ref, sfin_ref, st_ref, *, L, n_chunks, bb):
    c = pl.program_id(1)

    @pl.when(c == 0)
    def _():
        st_ref[...] = s0_ref[...]

    hk = H_A * DK_A
    row = lax.broadcasted_iota(jnp.int32, (L, L), 0)
    col = lax.broadcasted_iota(jnp.int32, (L, L), 1)
    causal = row >= col
    tri = jnp.where(causal, 1.0, 0.0).astype(BF16)
    nt = (((1,), (1,)), ((), ()))
    tn_dims = (((0,), (0,)), ((), ()))

    chains = [(b, h) for b in range(bb) for h in range(H_A)]
    bcum = []
    for b in range(bb):
        glow = qkg_ref[b, :, 2 * hk:2 * hk + GATE_PAD].astype(BF16)
        gate = jnp.dot(glow, wup_ref[...], preferred_element_type=F32) + bg_ref[...]
        logg = _log_sigmoid(gate) * (1.0 / GATE_TEMP)
        hi = logg.astype(BF16)
        r1 = logg - hi.astype(F32)
        mid = r1.astype(BF16)
        lo = (r1 - mid.astype(F32)).astype(BF16)
        bcum.append(jnp.dot(tri, hi, preferred_element_type=F32)
                    + jnp.dot(tri, mid, preferred_element_type=F32)
                    + jnp.dot(tri, lo, preferred_element_type=F32))
    q_dec, k_dec, k_upd, decay = {}, {}, {}, {}
    for (b, h) in chains:
        bh = bcum[b][:, h * DK_A:(h + 1) * DK_A]
        qh = qkg_ref[b, :, h * DK_A:(h + 1) * DK_A]
        kh = qkg_ref[b, :, hk + h * DK_A:hk + (h + 1) * DK_A]
        b_last = bh[L - 1:L, :]
        q_dec[b, h] = ((qh * (DK_A ** -0.5)) * jnp.exp(bh)).astype(BF16)
        k_dec[b, h] = (kh * jnp.exp(-bh)).astype(BF16)
        k_upd[b, h] = (kh * jnp.exp(b_last - bh)).astype(BF16)
        decay[b, h] = jnp.exp(b_last)
    att = {}
    for (b, h) in chains:
        a = lax.dot_general(q_dec[b, h], k_dec[b, h], nt, preferred_element_type=F32)
        att[b, h] = jnp.where(causal, a, 0.0).astype(BF16)
    outs = {}
    for (b, h) in chains:
        vh = v_ref[b, :, h * DV_A:(h + 1) * DV_A]
        st = st_ref[b, h]
        outs[b, h] = (lax.dot_general(q_dec[b, h], st.astype(BF16), nt, preferred_element_type=F32)
                      + jnp.dot(att[b, h], vh, preferred_element_type=F32))
        st_ref[b, h] = st * decay[b, h] + lax.dot_general(
            vh, k_upd[b, h], tn_dims, preferred_element_type=F32)
    for (b, h) in chains:
        vs = slice(h * DV_A, (h + 1) * DV_A)
        o = outs[b, h]
        on = o * lax.rsqrt(jnp.mean(o * o, -1, keepdims=True) + LN_EPS) * gn_ref[h:h + 1, :]
        o_ref[b, :, vs] = (on * jax.nn.silu(og_ref[b, :, vs])).astype(BF16)

    @pl.when(c == n_chunks - 1)
    def _():
        sfin_ref[...] = st_ref[...]


def _gla(qkg, v_a, og, wup, bg, s0_t, gn, *, B, T):
    L = min(T, CHUNK)
    n = T // L
    bb = min(B, 4)
    hv = H_A * DV_A
    seq = lambda width: pl.BlockSpec((bb, L, width), lambda g, c: (g, c, 0))
    const2 = lambda g, c: (0, 0)
    state = pl.BlockSpec((bb, H_A, DV_A, DK_A), lambda g, c: (g, 0, 0, 0))
    o, s_fin = pl.pallas_call(
        functools.partial(_gla_kernel, L=L, n_chunks=n, bb=bb),
        grid=(B // bb, n),
        in_specs=[seq(QKG_W), seq(hv), seq(hv),
                  pl.BlockSpec(wup.shape, const2), pl.BlockSpec(bg.shape, const2),
                  state, pl.BlockSpec(gn.shape, const2)],
        out_specs=[seq(hv), state],
        out_shape=[jax.ShapeDtypeStruct((B, T, hv), BF16),
                   jax.ShapeDtypeStruct((B, H_A, DV_A, DK_A), F32)],
        scratch_shapes=[pltpu.VMEM((bb, H_A, DV_A, DK_A), F32)],
        compiler_params=_cparams(2),
        name="gla",
    )(qkg.reshape(B, T, QKG_W), v_a.reshape(B, T, hv), og.reshape(B, T, hv), wup, bg, s0_t, gn)
    return o.reshape(B * T, hv), s_fin


def _band_attn_kernel(q_ref, kv_ref, bias_ref, o_ref):
    qb = pl.program_id(1)
    start = pl.multiple_of(jnp.maximum(qb * ATT_QB - BAND_PAST, 0), ATT_QB)
    hd = H_B * DH_B
    nt = (((1,), (1,)), ((), ()))

    def logits2(h):
        q = q_ref[:, h * DH_B:(h + 1) * DH_B]
        k = kv_ref[pl.ds(start, ATT_KW), h * DH_B:(h + 1) * DH_B]
        return (lax.dot_general(q, k, nt, preferred_element_type=F32) * (DH_B ** -0.5 * LOG2E)
                + bias_ref[0, h])

    def weights(t):
        e = jnp.exp2(t - jnp.max(t, -1, keepdims=True))
        return e.astype(BF16), jnp.sum(e, -1, keepdims=True)

    def output(h, e, den):
        v = kv_ref[pl.ds(start, ATT_KW), hd + h * DH_B:hd + (h + 1) * DH_B]
        o = jnp.dot(e, v, preferred_element_type=F32) / den
        o_ref[:, h * DH_B:(h + 1) * DH_B] = o.astype(BF16)

    t_next = logits2(0)
    for h in range(H_B):
        t_cur = t_next
        if h + 1 < H_B:
            t_next = logits2(h + 1)
        output(h, *weights(t_cur))


def _band_attn(q_b, kv_b, bias_blocks, *, B, T):
    assert T % ATT_QB == 0 and T >= ATT_KW
    n = T // ATT_QB
    hd = H_B * DH_B
    last_bias = bias_blocks.shape[0] - 1
    return pl.pallas_call(
        _band_attn_kernel,
        grid=(B, n),
        in_specs=[
            pl.BlockSpec((ATT_QB, hd), lambda b, c: (b * n + c, 0)),
            pl.BlockSpec((T, 2 * hd), lambda b, c: (b, 0), pipeline_mode=pl.Buffered(1)),
            pl.BlockSpec((1, H_B, ATT_QB, ATT_KW), lambda b, c: (jnp.minimum(c, last_bias), 0, 0, 0)),
        ],
        out_specs=pl.BlockSpec((ATT_QB, hd), lambda b, c: (b * n + c, 0)),
        out_shape=jax.ShapeDtypeStruct((B * T, hd), BF16),
        compiler_params=_cparams(2),
        name="band_attn",
    )(q_b, kv_b, bias_blocks)


def _attn_step_kernel(q_ref, kv_ref, ck_ref, cv_ref, bias_c_ref, bias_n_ref, o_ref):
    hd = H_B * DH_B
    lc = ck_ref.shape[0] // H_B
    nt = (((1,), (1,)), ((), ()))
    scale = DH_B ** -0.5
    for h in range(H_B):
        hs = slice(h * DH_B, (h + 1) * DH_B)
        q = q_ref[:, hs].astype(BF16)
        kc = ck_ref[pl.ds(h, lc, stride=H_B), :].astype(BF16)
        vc = cv_ref[pl.ds(h, lc, stride=H_B), :].astype(BF16)
        kn = kv_ref[:, hs].astype(BF16)
        vn = kv_ref[:, hd + h * DH_B:hd + (h + 1) * DH_B].astype(BF16)
        sc = lax.dot_general(q, kc, nt, preferred_element_type=F32) * scale + bias_c_ref[h]
        sn = lax.dot_general(q, kn, nt, preferred_element_type=F32) * scale + bias_n_ref[h]
        m = jnp.maximum(jnp.max(sc, -1, keepdims=True), jnp.max(sn, -1, keepdims=True))
        ec = jnp.exp(sc - m)
        en = jnp.exp(sn - m)
        den = jnp.sum(ec, -1, keepdims=True) + jnp.sum(en, -1, keepdims=True)
        o = (jnp.dot((ec / den).astype(BF16), vc, preferred_element_type=F32)
             + jnp.dot((en / den).astype(BF16), vn, preferred_element_type=F32))
        o_ref[:, hs] = o.astype(BF16)


def _attn_step(q_b, kv_b, cache_k, cache_v, bias_c, bias_n, *, B, T):
    hd = H_B * DH_B
    lc = cache_k.shape[2]
    ck = cache_k.reshape(1, B, lc * H_B, DH_B)
    cv = cache_v.reshape(1, B, lc * H_B, DH_B)
    cache = pl.BlockSpec((None, None, lc * H_B, DH_B), lambda b: (0, b, 0, 0))
    return pl.pallas_call(
        _attn_step_kernel,
        grid=(B,),
        in_specs=[
            pl.BlockSpec((T, hd), lambda b: (b, 0)),
            pl.BlockSpec((T, 2 * hd), lambda b: (b, 0)),
            cache, cache,
            pl.BlockSpec(bias_c.shape, lambda b: (0, 0, 0)),
            pl.BlockSpec(bias_n.shape, lambda b: (0, 0, 0)),
        ],
        out_specs=pl.BlockSpec((T, hd), lambda b: (b, 0)),
        out_shape=jax.ShapeDtypeStruct((B * T, hd), BF16),
        compiler_params=_cparams(1),
        name="attn_step",
    )(q_b, kv_b, ck, cv, bias_c, bias_n)


def _gmlp_kernel(xb_ref, x_ref, win_ref, lvg_ref, lvb_ref, ws_ref, bs_ref, wout_ref, g_ref, b_ref,
                 y_ref, yb_ref, *maybe_v_ref, tm, L):
    row = lax.broadcasted_iota(jnp.int32, (L, L), 0)
    col = lax.broadcasted_iota(jnp.int32, (L, L), 1)
    causal = row >= col
    gw = DC // G_C
    wms = [jnp.where(causal, ws_ref[g], 0.0).astype(BF16) for g in range(G_C)]
    sub = max(ROW_SUB, L)
    for r in range(tm // sub):
        rows = slice(r * sub, (r + 1) * sub)
        uv = jax.nn.gelu(jnp.dot(xb_ref[rows, :], win_ref[...], preferred_element_type=F32))
        u = uv[:, :DC]
        v = _layer_norm(uv[:, DC:], lvg_ref[...], lvb_ref[...])
        if maybe_v_ref:
            maybe_v_ref[0][rows, :] = v
        vb = v.astype(BF16)
        chunks = []
        for c in range(sub // L):
            cr = slice(c * L, (c + 1) * L)
            groups = []
            for g in range(G_C):
                cs = slice(g * gw, (g + 1) * gw)
                sv = jnp.dot(wms[g], vb[cr, cs], preferred_element_type=F32) + bs_ref[:, g:g + 1]
                groups.append((u[cr, cs] * sv).astype(BF16))
            chunks.append(jnp.concatenate(groups, axis=1))
        gated = chunks[0] if len(chunks) == 1 else jnp.concatenate(chunks, axis=0)
        h = jnp.dot(gated, wout_ref[...], preferred_element_type=F32)
        y = _layer_norm(ALPHA * x_ref[rows, :] + h, g_ref[...], b_ref[...])
        y_ref[rows, :] = y
        yb_ref[rows, :] = y.astype(BF16)


def _gmlp(xb, x, O, g, b, *, T, tm, emit_v):
    m = x.shape[0]
    L = min(T, CHUNK_C)
    ws = O["ws"][:, :L, :L]
    bs_t = jnp.transpose(O["bs"][:, :L])
    row = pl.BlockSpec((tm, D_MODEL), lambda i: (i, 0))
    vec = pl.BlockSpec((1, D_MODEL), lambda i: (0, 0))
    return pl.pallas_call(
        functools.partial(_gmlp_kernel, tm=tm, L=L),
        grid=(m // tm,),
        in_specs=[row, row, _resident(O["w_in"].shape), vec, vec,
                  pl.BlockSpec(ws.shape, lambda i: (0, 0, 0)), pl.BlockSpec(bs_t.shape, lambda i: (0, 0)),
                  _resident(O["w_out"].shape), vec, vec],
        out_specs=[row] * (3 if emit_v else 2),
        out_shape=[jax.ShapeDtypeStruct((m, D_MODEL), F32), jax.ShapeDtypeStruct((m, D_MODEL), BF16)]
        + ([jax.ShapeDtypeStruct((m, DC), F32)] if emit_v else []),
        compiler_params=_cparams(1),
        name="gmlp",
    )(xb, x, O["w_in"], O["ln_v_g"].reshape(1, DC), O["ln_v_b"].reshape(1, DC), ws, bs_t,
      O["w_out"], g.reshape(1, D_MODEL), b.reshape(1, D_MODEL))


def _ffn_up_kernel(x_ref, w1_ref, w2_ref, hist_ref, cw_ref, cb_ref, h_ref, st_ref,
                   wb_ref, carry_ref, *, tm, seg, tiles_per_seq):
    j = pl.program_id(0)
    i = pl.program_id(1)
    tn = FF_TILE

    @pl.when(i == 0)
    def _():
        keep = j * tn + lax.broadcasted_iota(jnp.int32, (1, tn), 1) < D_FF
        wb_ref[:, :tn] = jnp.where(keep, w1_ref[...], 0.0).astype(BF16)
        wb_ref[:, tn:] = jnp.where(keep, w2_ref[...], 0.0).astype(BF16)
        carry_ref[...] = jnp.zeros_like(carry_ref)

    n_seg = tm // seg
    rg = min(tm, 256)
    piece = min(seg, rg)
    seq_start = i % tiles_per_seq == 0
    sub_i = lax.broadcasted_iota(jnp.int32, (SUBLANE, MXU_N), 0)
    for c in range(tn // MXU_N):
        cs = slice(c * MXU_N, (c + 1) * MXU_N)
        prev = None
        for r in range(tm // rg):
            x = x_ref[r * rg:(r + 1) * rg, :]
            u = jnp.dot(x, wb_ref[:, c * MXU_N:(c + 1) * MXU_N], preferred_element_type=F32)
            z = jnp.dot(x, wb_ref[:, tn + c * MXU_N:tn + (c + 1) * MXU_N], preferred_element_type=F32)
            for p in range(rg // piece):
                row0 = r * rg + p * piece
                s = row0 // seg
                if row0 % seg == 0:
                    if n_seg == 1:
                        prev = jnp.where(seq_start, hist_ref[0, :, cs], carry_ref[:, cs])
                    else:
                        prev = hist_ref[s, :, cs]
                up = u[p * piece:(p + 1) * piece]
                zp = z[p * piece:(p + 1) * piece]
                u1 = pltpu.roll(up, 1, 0)
                u2 = pltpu.roll(up, 2, 0)
                h1 = jnp.where(sub_i < 1, pltpu.roll(prev, 1, 0), u1[:SUBLANE])
                h2 = jnp.where(sub_i < 2, pltpu.roll(prev, 2, 0), u2[:SUBLANE])
                u1 = jnp.concatenate([h1, u1[SUBLANE:]], axis=0)
                u2 = jnp.concatenate([h2, u2[SUBLANE:]], axis=0)
                cv = u2 * cw_ref[0:1, cs] + u1 * cw_ref[1:2, cs] + up * cw_ref[2:3, cs]
                h_ref[row0:row0 + piece, cs] = (jax.nn.gelu(cv + cb_ref[:, cs]) * zp).astype(BF16)
                prev = up[piece - SUBLANE:]
                if (row0 + piece) % seg == 0:
                    st_ref[s, :, cs] = prev
                    if n_seg == 1:
                        carry_ref[:, cs] = prev


def _ffn_up(xb, w1_all, w2_all, layer, hist8, cw, cb, *, T, tm):
    m = xb.shape[0]
    tn = FF_TILE
    seg = min(T, tm)
    n_seg = tm // seg
    tps = T // seg
    wspec = pl.BlockSpec((None, D_MODEL, tn), lambda j, i: (layer, 0, j))
    return pl.pallas_call(
        functools.partial(_ffn_up_kernel, tm=tm, seg=seg, tiles_per_seq=tps),
        grid=(N_FF_TILES, m // tm),
        in_specs=[
            pl.BlockSpec((tm, D_MODEL), lambda j, i: (i, 0)),
            wspec, wspec,
            pl.BlockSpec((n_seg, SUBLANE, tn), lambda j, i: (i // tps, 0, j)),
            pl.BlockSpec((CONV_W, tn), lambda j, i: (0, j)),
            pl.BlockSpec((1, tn), lambda j, i: (0, j)),
        ],
        out_specs=[
            pl.BlockSpec((tm, tn), lambda j, i: (i, j)),
            pl.BlockSpec((n_seg, SUBLANE, tn), lambda j, i: (i, 0, j)),
        ],
        out_shape=[
            jax.ShapeDtypeStruct((m, D_FF_PAD), BF16),
            jax.ShapeDtypeStruct((m // seg, SUBLANE, D_FF_PAD), F32),
        ],
        scratch_shapes=[pltpu.VMEM((D_MODEL, 2 * tn), BF16), pltpu.VMEM((SUBLANE, tn), F32)],
        compiler_params=_cparams(2),
        name="ffn_up",
    )(xb, w1_all, w2_all, hist8, cw, cb)


def _ffn_down_kernel(h_ref, w_ref, res_ref, g_ref, b_ref, y_ref, yb_ref, *, tm):
    for r in range(tm // ROW_SUB):
        rows = slice(r * ROW_SUB, (r + 1) * ROW_SUB)
        acc = jnp.dot(h_ref[rows, :], w_ref[...], preferred_element_type=F32)
        y = _layer_norm(ALPHA * res_ref[rows, :] + acc, g_ref[...], b_ref[...])
        y_ref[rows, :] = y
        yb_ref[rows, :] = y.astype(BF16)


def _ffn_down(h, w3, res, g, b, *, tm):
    m, n = res.shape
    row = lambda width: pl.BlockSpec((tm, width), lambda i: (i, 0))
    vec = pl.BlockSpec((1, n), lambda i: (0, 0))
    return pl.pallas_call(
        functools.partial(_ffn_down_kernel, tm=tm),
        grid=(m // tm,),
        in_specs=[row(D_FF_PAD), _resident(w3.shape), row(n), vec, vec],
        out_specs=[row(n), row(n)],
        out_shape=[jax.ShapeDtypeStruct((m, n), F32), jax.ShapeDtypeStruct((m, n), BF16)],
        compiler_params=_cparams(1),
        name="ffn_down",
    )(h, w3, res, g.reshape(1, n), b.reshape(1, n))


def _cast_pad_rows_kernel(w_ref, o_ref, *, rows_valid, tr):
    k = pl.program_id(0)
    keep = k * tr + lax.broadcasted_iota(jnp.int32, (tr, 1), 0) < rows_valid
    o_ref[...] = jnp.where(keep, w_ref[...], 0.0).astype(BF16)


def _cast_pad_rows(w_all, layer, rows_pad, *, tr=FF_TILE):
    _, rows, n = w_all.shape
    return pl.pallas_call(
        functools.partial(_cast_pad_rows_kernel, rows_valid=rows, tr=tr),
        grid=(rows_pad // tr,),
        in_specs=[pl.BlockSpec((None, tr, n), lambda k: (layer, k, 0))],
        out_specs=pl.BlockSpec((tr, n), lambda k: (k, 0)),
        out_shape=jax.ShapeDtypeStruct((rows_pad, n), BF16),
        compiler_params=_cparams(1),
        name="cast_pad_rows",
    )(w_all)


def _split_w_in_kernel(w_ref, qkg_ref, v_ref, og_ref, qb_ref, kv_ref):
    hk, hv, hd = H_A * DK_A, H_A * DV_A, H_B * DH_B
    qkg_ref[:, :2 * hk] = w_ref[:, :2 * hk].astype(BF16)
    o = 2 * hk
    v_ref[...] = w_ref[:, o:o + hv].astype(BF16)
    o += hv
    g = w_ref[:, o:o + GATE_PAD]
    lane = lax.broadcasted_iota(jnp.int32, g.shape, 1)
    qkg_ref[:, 2 * hk:] = jnp.where(lane < GATE_RANK, g, 0.0).astype(BF16)
    o += GATE_RANK
    og_ref[...] = w_ref[:, o:o + hv].astype(BF16)
    o += hv
    qb_ref[...] = w_ref[:, o:o + hd].astype(BF16)
    o += hd
    kv_ref[...] = w_ref[:, o:o + 2 * hd].astype(BF16)


def _split_w_in(w_in_all, *, tr=256):
    _, k, n = w_in_all.shape
    hv, hd = H_A * DV_A, H_B * DH_B
    widths = [QKG_W, hv, hv, hd, 2 * hd]
    return pl.pallas_call(
        _split_w_in_kernel,
        grid=(k // tr,),
        in_specs=[pl.BlockSpec((None, tr, n), lambda i: (0, i, 0))],
        out_specs=[pl.BlockSpec((tr, w), lambda i: (i, 0)) for w in widths],
        out_shape=[jax.ShapeDtypeStruct((k, w), BF16) for w in widths],
        compiler_params=_cparams(1),
        name="split_w_in",
    )(w_in_all)


def _prep_even(w_in_all, w_gate_up, b_gate, w_out):
    hk, hv = H_A * DK_A, H_A * DV_A
    w_qkg, w_v, w_og, w_qb, w_kv = _split_w_in(w_in_all)
    wup = jnp.pad(w_gate_up, ((0, GATE_PAD - GATE_RANK), (0, 0)))
    return dict(
        w_qkg=w_qkg, w_v=w_v, w_og=w_og, w_qb=w_qb, w_kv=w_kv,
        wup=wup.astype(BF16), bg=b_gate.reshape(1, hk),
        w_out_a=w_out[:hv].astype(BF16), w_out_b=w_out[hv:].astype(BF16),
    )


def _prep_ffn(cw, cb, w3_all, layer):
    pad = D_FF_PAD - D_FF
    return dict(layer=layer,
                cw=jnp.pad(cw, ((0, 0), (0, pad))),
                cb=jnp.pad(cb, (0, pad)).reshape(1, D_FF_PAD),
                w3=_cast_pad_rows(w3_all, layer, D_FF_PAD))


def _toeplitz_kernel(ext_ref, o_ref):
    heads, rows, width = o_ref.shape
    for h in range(heads):
        o_ref[h] = pltpu.roll(jnp.broadcast_to(ext_ref[h], (rows, width)), 0, 1, stride=1, stride_axis=0)


def _rel_toeplitz(table, n_rows, n_cols, offset):
    heads = table.shape[0]
    pad = -(-n_rows // LANE) * LANE
    width = -(-(pad + n_cols) // LANE) * LANE
    n_hi = offset + pad - REL_CLIP
    assert n_hi >= 0
    ext = jnp.concatenate([
        jnp.broadcast_to(table[:, -1:], (heads, n_hi)), table[:, ::-1],
        jnp.broadcast_to(table[:, :1], (heads, width))], axis=1)[:, :width]
    toe = pl.pallas_call(
        _toeplitz_kernel,
        grid=(1,),
        in_specs=[pl.BlockSpec((heads, 1, width), lambda i: (0, 0, 0))],
        out_specs=pl.BlockSpec((heads, n_rows, width), lambda i: (0, 0, 0)),
        out_shape=jax.ShapeDtypeStruct((heads, n_rows, width), F32),
        compiler_params=_cparams(1),
        name="rel_toeplitz",
    )(ext.astype(F32).reshape(heads, 1, width))
    return toe, pad


def _prompt_bias(table):
    n_special = BAND_PAST // ATT_QB
    toe, pad = _rel_toeplitz(table, ATT_QB, ATT_KW + BAND_PAST, BAND_PAST)
    r = jnp.arange(ATT_QB)[:, None]
    j = jnp.arange(ATT_KW)[None, :]
    out = []
    for qb in range(n_special + 1):
        q_chunk = (qb * ATT_QB + r) // CHUNK
        valid = (j >= q_chunk * CHUNK - BAND_PAST) & (j < (q_chunk + 1) * CHUNK)
        shift = pad + (n_special - qb) * ATT_QB
        out.append(jnp.where(valid[None], toe[:, :, shift:shift + ATT_KW] * LOG2E, NEG_INF))
    return jnp.stack(out)


def _step_bias(table, T, lc):
    toe, pad = _rel_toeplitz(table, T, lc + T, lc)
    return toe[:, :, pad:pad + lc], toe[:, :, pad + lc:pad + lc + T]


def _conv_ffn(x, xb, F, ln_g, ln_b, hist, *, B, T, tm_up, tm_down):
    pad = D_FF_PAD - D_FF
    keep = CONV_W - 1
    hist8 = jnp.pad(hist, ((0, 0), (SUBLANE - keep, 0), (0, pad)))
    h, st8 = _ffn_up(xb, F["w1_all"], F["w2_all"], F["layer"], hist8, F["cw"], F["cb"], T=T, tm=tm_up)
    new_state = st8.reshape(B, -1, SUBLANE, D_FF_PAD)[:, -1, SUBLANE - keep:, :D_FF]
    y, yb = _ffn_down(h, F["w3"], x, ln_g, ln_b, tm=tm_down)
    return y, yb, new_state


def _trunk(x3, E, Fs, O, norms, state, bias):
    B, T, _ = x3.shape
    m = B * T
    tm = min(512, m)
    tm_up = min(2048, m)
    tm_small = min(256, m)
    x = x3.reshape(m, D_MODEL)
    hv, hd = H_A * DV_A, H_B * DH_B
    prompt = state is None

    qkg, xb = _linear(x, E["w_qkg"], out_dtype=F32, tm=tm, emit_xb=True)
    v_a = _linear(xb, E["w_v"], out_dtype=BF16, tm=tm)
    og = _linear(xb, E["w_og"], out_dtype=F32, tm=tm)
    q_b = _linear(xb, E["w_qb"], out_dtype=BF16, tm=tm)
    if prompt:
        s0_t = jnp.zeros((B, H_A, DV_A, DK_A), F32)
        conv_hist = [jnp.zeros((B, CONV_W - 1, D_FF), F32)] * DEPTH
    else:
        cache_k, cache_v, state_gla, state_conv = state
        s0_t = jnp.swapaxes(state_gla[0], -1, -2)
        conv_hist = [state_conv[i] for i in range(DEPTH)]
    o_a, s_fin_t = _gla(qkg, v_a, og, E["wup"], E["bg"], s0_t, E["gn"], B=B, T=T)
    new_gla = jnp.swapaxes(s_fin_t, -1, -2)[None]
    if prompt:
        kv_b = _linear(xb, E["w_kv"], out_dtype=BF16, tm=tm)
        o_b = _band_attn(q_b, kv_b, bias, B=B, T=T)
        keep = min(BAND_PAST, T)
        blocks_per_seq = T // keep
        kv_tail = _linear(xb, E["w_kv"], out_dtype=F32, tm=keep, n_row_blocks=B,
                          x_block_index=lambda i: i * blocks_per_seq + blocks_per_seq - 1)
        kv_rows = kv_tail.reshape(B, keep, 2, H_B, DH_B)
    else:
        kv_f = _linear(xb, E["w_kv"], out_dtype=F32, tm=tm)
        o_b = _attn_step(q_b, kv_f, cache_k, cache_v, bias[0], bias[1], B=B, T=T)
        kv_rows = kv_f.reshape(B, T, 2, H_B, DH_B)
    new_k = kv_rows[:, :, 0][None]
    new_v = kv_rows[:, :, 1][None]
    x, xb = _proj_ln(o_a, o_b, E["w_out_a"], E["w_out_b"], x, norms["ln1_g"][0], norms["ln1_b"][0], tm=tm)
    x, xb, conv0 = _conv_ffn(x, xb, Fs[0], norms["ln2_g"][0], norms["ln2_b"][0], conv_hist[0],
                             B=B, T=T, tm_up=tm_up, tm_down=tm_small)

    x, xb, *maybe_v = _gmlp(xb, x, O, norms["ln1_g"][1], norms["ln1_b"][1], T=T,
                            tm=tm_small if not prompt else tm, emit_v=not prompt)
    x, xb, conv1 = _conv_ffn(x, xb, Fs[1], norms["ln2_g"][1], norms["ln2_b"][1], conv_hist[1],
                             B=B, T=T, tm_up=tm_up, tm_down=tm_small)

    y = x.reshape(B, T, D_MODEL)
    new_conv = jnp.stack([conv0, conv1])
    mlp_v = maybe_v[0].reshape(B, T, DC)[None] if maybe_v else None
    return y, new_k, new_v, new_gla, new_conv, mlp_v


def kernel(x_prompt, x_sample, cache_attn_k, cache_attn_v, state_gla, state_ffn_conv, w_in_even, w_gate_up, b_gate, gla_norm_g, rel_bias, w_out_even, w_in_odd, ln_v_g, ln_v_b, w_spatial, b_spatial, w_out_odd, ffn_w1, ffn_w2, ffn_conv_w, ffn_conv_b, ffn_w3, ln1_g, ln1_b, ln2_g, ln2_b):
    E = _prep_even(w_in_even, w_gate_up[0], b_gate[0], w_out_even[0])
    E["gn"] = gla_norm_g[0]
    Fs = [dict(_prep_ffn(ffn_conv_w[i], ffn_conv_b[i], ffn_w3, i), w1_all=ffn_w1, w2_all=ffn_w2)
          for i in range(DEPTH)]
    O = dict(w_in=w_in_odd[0].astype(BF16), ln_v_g=ln_v_g[0], ln_v_b=ln_v_b[0],
             ws=w_spatial[0], bs=b_spatial[0], w_out=w_out_odd[0].astype(BF16))
    norms = dict(ln1_g=ln1_g, ln1_b=ln1_b, ln2_g=ln2_g, ln2_b=ln2_b)

    bias_p = _prompt_bias(rel_bias[0])
    y_p, k_p, v_p, gla_p, conv_p, _ = _trunk(x_prompt, E, Fs, O, norms, None, bias_p)

    bias_s = _step_bias(rel_bias[0], x_sample.shape[1], cache_attn_k.shape[2])
    y_s, k_s, v_s, gla_s, conv_s, mlp_v_s = _trunk(
        x_sample, E, Fs, O, norms, (cache_attn_k, cache_attn_v, state_gla, state_ffn_conv), bias_s)
    return (y_p, y_s, k_p, v_p, gla_p, conv_p, k_s, v_s, gla_s, conv_s, mlp_v_s)
```

```python
import functools

import jax
import jax.numpy as jnp
from jax import lax
from jax.experimental import pallas as pl
from jax.experimental.pallas import tpu as pltpu

F32 = jnp.float32
BF16 = jnp.bfloat16

D_MODEL = 2048
DEPTH = 2
CHUNK = 64
H_A = 4
DK_A = D_MODEL // 16
DV_A = D_MODEL // 8
GATE_RANK = 16
GATE_TEMP = 16.0
H_B = 8
DH_B = D_MODEL // 16
N_PREV_CHUNKS = 8
BAND_PAST = N_PREV_CHUNKS * CHUNK
REL_CLIP = 128
CHUNK_C = 128
DC = D_MODEL
G_C = 8
D_FF = ((8 * D_MODEL // 3 + 127) // 128) * 128
CONV_W = 3
ALPHA = (2 * DEPTH) ** 0.25
LN_EPS = 1e-5
NEG_INF = -1e30
PAST_LEN = 2048
LOG2E = 1.4426950408889634

LANE = 128
SUBLANE = 8
MXU_N = 256
FF_TILE = 512
D_FF_PAD = ((D_FF + FF_TILE - 1) // FF_TILE) * FF_TILE
N_FF_TILES = D_FF_PAD // FF_TILE
GATE_PAD = LANE
QKG_W = 2 * H_A * DK_A + GATE_PAD
ATT_QB = 4 * CHUNK
ATT_KW = BAND_PAST + ATT_QB
ROW_SUB = 128
VMEM_LIMIT = 56 * 1024 * 1024
VMEM_LIMIT_FFN_DOWN = 58 * 1024 * 1024


def _cparams(n_axes, vmem=VMEM_LIMIT, **kw):
    return pltpu.CompilerParams(dimension_semantics=("arbitrary",) * n_axes,
                                vmem_limit_bytes=vmem, **kw)


def _resident(shape):
    nd = len(shape)
    return pl.BlockSpec(shape, lambda *_: (0,) * nd, pipeline_mode=pl.Buffered(1))


def _layer_norm(xf, g, b):
    mu = jnp.mean(xf, -1, keepdims=True)
    xc = xf - mu
    var = jnp.mean(xc * xc, -1, keepdims=True)
    y = xc * lax.rsqrt(var + LN_EPS)
    return y * g + b


def _log_sigmoid(x):
    return jnp.minimum(x, 0.0) - jnp.log1p(jnp.exp(-jnp.abs(x)))


def _linear_kernel(x_ref, w_ref, o_ref, *maybe_xb_ref):
    xb = x_ref[...].astype(BF16)
    o_ref[...] = jnp.dot(xb, w_ref[...], preferred_element_type=F32).astype(o_ref.dtype)
    if maybe_xb_ref:
        maybe_xb_ref[0][...] = xb


def _linear(x, w, *, out_dtype, tm, emit_xb=False, x_block_index=None, n_row_blocks=None):
    m, k = x.shape
    n = w.shape[1]
    n_rows = (m // tm) if n_row_blocks is None else n_row_blocks
    xmap = (lambda i: (i, 0)) if x_block_index is None else (lambda i: (x_block_index(i), 0))
    out_shape = [jax.ShapeDtypeStruct((n_rows * tm, n), out_dtype)]
    out_specs = [pl.BlockSpec((tm, n), lambda i: (i, 0))]
    if emit_xb:
        out_shape.append(jax.ShapeDtypeStruct((m, k), BF16))
        out_specs.append(pl.BlockSpec((tm, k), lambda i: (i, 0)))
    res = pl.pallas_call(
        _linear_kernel,
        grid=(n_rows,),
        in_specs=[pl.BlockSpec((tm, k), xmap), _resident(w.shape)],
        out_specs=out_specs,
        out_shape=out_shape,
        compiler_params=_cparams(1),
        name="linear",
    )(x, w)
    return res if emit_xb else res[0]


def _proj_ln_kernel(xa_ref, xb_ref, wa_ref, wb_ref, res_ref, g_ref, b_ref, y_ref, yb_ref, *, tm):
    for r in range(tm // ROW_SUB):
        rows = slice(r * ROW_SUB, (r + 1) * ROW_SUB)
        h = (jnp.dot(xa_ref[rows, :], wa_ref[...], preferred_element_type=F32)
             + jnp.dot(xb_ref[rows, :], wb_ref[...], preferred_element_type=F32))
        y = _layer_norm(ALPHA * res_ref[rows, :] + h, g_ref[...], b_ref[...])
        y_ref[rows, :] = y
        yb_ref[rows, :] = y.astype(BF16)


def _proj_ln(xa, xb, wa, wb, res, g, b, *, tm):
    m, n = res.shape
    row = lambda width: pl.BlockSpec((tm, width), lambda i: (i, 0))
    vec = pl.BlockSpec((1, n), lambda i: (0, 0))
    return pl.pallas_call(
        functools.partial(_proj_ln_kernel, tm=tm),
        grid=(m // tm,),
        in_specs=[row(xa.shape[1]), row(xb.shape[1]), _resident(wa.shape), _resident(wb.shape),
                  row(n), vec, vec],
        out_specs=[row(n), row(n)],
        out_shape=[jax.ShapeDtypeStruct((m, n), F32), jax.ShapeDtypeStruct((m, n), BF16)],
        compiler_params=_cparams(1),
        name="proj_ln",
    )(xa, xb, wa, wb, res, g.reshape(1, n), b.reshape(1, n))


def _gla_kernel(qkg_ref, v_ref, og_ref, wup_ref, bg_ref, s0_ref, gn_ref,
                o_ref, sfin_ref, st_ref, *, L, n_steps, bb, cps):
    c = pl.program_id(1)

    @pl.when(c == 0)
    def _():
        st_ref[...] = s0_ref[...]

    hk = H_A * DK_A
    row = lax.broadcasted_iota(jnp.int32, (L, L), 0)
    col = lax.broadcasted_iota(jnp.int32, (L, L), 1)
    causal = row >= col
    tri = jnp.where(causal, 1.0, 0.0).astype(BF16)
    nt = (((1,), (1,)), ((), ()))
    tn_dims = (((0,), (0,)), ((), ()))

    chains = [(ci, b, h) for ci in range(cps) for b in range(bb) for h in range(H_A)]
    rows = lambda ci: slice(ci * L, (ci + 1) * L)
    bcum = {}
    for ci in range(cps):
        for b in range(bb):
            glow = qkg_ref[b, rows(ci), 2 * hk:2 * hk + GATE_PAD].astype(BF16)
            gate = jnp.dot(glow, wup_ref[...], preferred_element_type=F32) + bg_ref[...]
            logg = _log_sigmoid(gate) * (1.0 / GATE_TEMP)
            hi = logg.astype(BF16)
            r1 = logg - hi.astype(F32)
            mid = r1.astype(BF16)
            lo = (r1 - mid.astype(F32)).astype(BF16)
            bcum[ci, b] = (jnp.dot(tri, hi, preferred_element_type=F32)
                           + jnp.dot(tri, mid, preferred_element_type=F32)
                           + jnp.dot(tri, lo, preferred_element_type=F32))
    q_dec, k_dec, k_upd, decay = {}, {}, {}, {}
    for (ci, b, h) in chains:
        bh = bcum[ci, b][:, h * DK_A:(h + 1) * DK_A]
        qh = qkg_ref[b, rows(ci), h * DK_A:(h + 1) * DK_A]
        kh = qkg_ref[b, rows(ci), hk + h * DK_A:hk + (h + 1) * DK_A]
        b_last = bh[L - 1:L, :]
        q_dec[ci, b, h] = ((qh * (DK_A ** -0.5)) * jnp.exp(bh)).astype(BF16)
        k_dec[ci, b, h] = (kh * jnp.exp(-bh)).astype(BF16)
        k_upd[ci, b, h] = (kh * jnp.exp(b_last - bh)).astype(BF16)
        decay[ci, b, h] = jnp.exp(b_last)
    att = {}
    for key in chains:
        a = lax.dot_general(q_dec[key], k_dec[key], nt, preferred_element_type=F32)
        att[key] = jnp.where(causal, a, 0.0).astype(BF16)
    outs = {}
    for (ci, b, h) in chains:
        key = (ci, b, h)
        vh = v_ref[b, rows(ci), h * DV_A:(h + 1) * DV_A]
        st = st_ref[b, h]
        outs[key] = (lax.dot_general(q_dec[key], st.astype(BF16), nt, preferred_element_type=F32)
                     + jnp.dot(att[key], vh, preferred_element_type=F32))
        st_ref[b, h] = st * decay[key] + lax.dot_general(
            vh, k_upd[key], tn_dims, preferred_element_type=F32)
    for (ci, b, h) in chains:
        vs = slice(h * DV_A, (h + 1) * DV_A)
        o = outs[ci, b, h]
        on = o * lax.rsqrt(jnp.mean(o * o, -1, keepdims=True) + LN_EPS) * gn_ref[h:h + 1, :]
        o_ref[b, rows(ci), vs] = (on * jax.nn.silu(og_ref[b, rows(ci), vs])).astype(BF16)

    @pl.when(c == n_steps - 1)
    def _():
        sfin_ref[...] = st_ref[...]


def _gla(qkg, v_a, og, wup, bg, s0_t, gn, *, B, T):
    L = min(T, CHUNK)
    n = T // L
    cps = 2 if n % 2 == 0 else 1
    bb = min(B, 4)
    hv = H_A * DV_A
    seq = lambda width: pl.BlockSpec((bb, cps * L, width), lambda g, c: (g, c, 0))
    const2 = lambda g, c: (0, 0)
    state = pl.BlockSpec((bb, H_A, DV_A, DK_A), lambda g, c: (g, 0, 0, 0))
    o, s_fin = pl.pallas_call(
        functools.partial(_gla_kernel, L=L, n_steps=n // cps, bb=bb, cps=cps),
        grid=(B // bb, n // cps),
        in_specs=[seq(QKG_W), seq(hv), seq(hv),
                  pl.BlockSpec(wup.shape, const2), pl.BlockSpec(bg.shape, const2),
                  state, pl.BlockSpec(gn.shape, const2)],
        out_specs=[seq(hv), state],
        out_shape=[jax.ShapeDtypeStruct((B, T, hv), BF16),
                   jax.ShapeDtypeStruct((B, H_A, DV_A, DK_A), F32)],
        scratch_shapes=[pltpu.VMEM((bb, H_A, DV_A, DK_A), F32)],
        compiler_params=_cparams(2),
        name="gla",
    )(qkg.reshape(B, T, QKG_W), v_a.reshape(B, T, hv), og.reshape(B, T, hv), wup, bg, s0_t, gn)
    return o.reshape(B * T, hv), s_fin


def _band_attn_kernel(q_ref, kv_ref, bias_ref, o_ref):
    qb = pl.program_id(1)
    start = pl.multiple_of(jnp.maximum(qb * ATT_QB - BAND_PAST, 0), ATT_QB)
    hd = H_B * DH_B
    nt = (((1,), (1,)), ((), ()))

    def logits2(h):
        q = q_ref[:, h * DH_B:(h + 1) * DH_B]
        k = kv_ref[pl.ds(start, ATT_KW), h * DH_B:(h + 1) * DH_B]
        return (lax.dot_general(q, k, nt, preferred_element_type=F32) * (DH_B ** -0.5 * LOG2E)
                + bias_ref[0, h])

    def weights(t):
        e = jnp.exp2(t - jnp.max(t, -1, keepdims=True))
        return e.astype(BF16), jnp.sum(e, -1, keepdims=True)

    def output(h, e, den):
        v = kv_ref[pl.ds(start, ATT_KW), hd + h * DH_B:hd + (h + 1) * DH_B]
        o = jnp.dot(e, v, preferred_element_type=F32) / den
        o_ref[:, h * DH_B:(h + 1) * DH_B] = o.astype(BF16)

    t_next = logits2(0)
    for h in range(H_B):
        t_cur = t_next
        if h + 1 < H_B:
            t_next = logits2(h + 1)
        output(h, *weights(t_cur))


def _band_attn(q_b, kv_b, bias_blocks, *, B, T):
    assert T % ATT_QB == 0 and T >= ATT_KW
    n = T // ATT_QB
    hd = H_B * DH_B
    last_bias = bias_blocks.shape[0] - 1
    return pl.pallas_call(
        _band_attn_kernel,
        grid=(B, n),
        in_specs=[
            pl.BlockSpec((ATT_QB, hd), lambda b, c: (b * n + c, 0)),
            pl.BlockSpec((T, 2 * hd), lambda b, c: (b, 0)),
            pl.BlockSpec((1, H_B, ATT_QB, ATT_KW), lambda b, c: (jnp.minimum(c, last_bias), 0, 0, 0)),
        ],
        out_specs=pl.BlockSpec((ATT_QB, hd), lambda b, c: (b * n + c, 0)),
        out_shape=jax.ShapeDtypeStruct((B * T, hd), BF16),
        compiler_params=_cparams(2),
        name="band_attn",
    )(q_b, kv_b, bias_blocks)


def _attn_step_kernel(q_ref, kv_ref, ck_ref, cv_ref, bias_c_ref, bias_n_ref, o_ref):
    hd = H_B * DH_B
    lc = ck_ref.shape[0] // H_B
    nt = (((1,), (1,)), ((), ()))
    scale = DH_B ** -0.5
    for h in range(H_B):
        hs = slice(h * DH_B, (h + 1) * DH_B)
        q = q_ref[:, hs].astype(BF16)
        kc = ck_ref[pl.ds(h, lc, stride=H_B), :].astype(BF16)
        vc = cv_ref[pl.ds(h, lc, stride=H_B), :].astype(BF16)
        kn = kv_ref[:, hs].astype(BF16)
        vn = kv_ref[:, hd + h * DH_B:hd + (h + 1) * DH_B].astype(BF16)
        sc = lax.dot_general(q, kc, nt, preferred_element_type=F32) * scale + bias_c_ref[h]
        sn = lax.dot_general(q, kn, nt, preferred_element_type=F32) * scale + bias_n_ref[h]
        m = jnp.maximum(jnp.max(sc, -1, keepdims=True), jnp.max(sn, -1, keepdims=True))
        ec = jnp.exp(sc - m)
        en = jnp.exp(sn - m)
        den = jnp.sum(ec, -1, keepdims=True) + jnp.sum(en, -1, keepdims=True)
        o = (jnp.dot((ec / den).astype(BF16), vc, preferred_element_type=F32)
             + jnp.dot((en / den).astype(BF16), vn, preferred_element_type=F32))
        o_ref[:, hs] = o.astype(BF16)


def _attn_step(q_b, kv_b, cache_k, cache_v, bias_c, bias_n, *, B, T):
    hd = H_B * DH_B
    lc = cache_k.shape[2]
    ck = cache_k.reshape(1, B, lc * H_B, DH_B)
    cv = cache_v.reshape(1, B, lc * H_B, DH_B)
    cache = pl.BlockSpec((None, None, lc * H_B, DH_B), lambda b: (0, b, 0, 0))
    return pl.pallas_call(
        _attn_step_kernel,
        grid=(B,),
        in_specs=[
            pl.BlockSpec((T, hd), lambda b: (b, 0)),
            pl.BlockSpec((T, 2 * hd), lambda b: (b, 0)),
            cache, cache,
            pl.BlockSpec(bias_c.shape, lambda b: (0, 0, 0)),
            pl.BlockSpec(bias_n.shape, lambda b: (0, 0, 0)),
        ],
        out_specs=pl.BlockSpec((T, hd), lambda b: (b, 0)),
        out_shape=jax.ShapeDtypeStruct((B * T, hd), BF16),
        compiler_params=_cparams(1),
        name="attn_step",
    )(q_b, kv_b, ck, cv, bias_c, bias_n)


def _gmlp_kernel(xb_ref, x_ref, win_ref, lvg_ref, lvb_ref, ws_ref, bs_ref, wout_ref, g_ref, b_ref,
                 y_ref, yb_ref, *maybe_v_ref, tm, L):
    row = lax.broadcasted_iota(jnp.int32, (L, L), 0)
    col = lax.broadcasted_iota(jnp.int32, (L, L), 1)
    causal = row >= col
    gw = DC // G_C
    wms = [jnp.where(causal, ws_ref[g], 0.0).astype(BF16) for g in range(G_C)]
    sub = max(ROW_SUB, L)
    for r in range(tm // sub):
        rows = slice(r * sub, (r + 1) * sub)
        uv = jax.nn.gelu(jnp.dot(xb_ref[rows, :], win_ref[...], preferred_element_type=F32))
        u = uv[:, :DC]
        v = _layer_norm(uv[:, DC:], lvg_ref[...], lvb_ref[...])
        if maybe_v_ref:
            maybe_v_ref[0][rows, :] = v
        vb = v.astype(BF16)
        chunks = []
        for c in range(sub // L):
            cr = slice(c * L, (c + 1) * L)
            groups = []
            for g in range(G_C):
                cs = slice(g * gw, (g + 1) * gw)
                sv = jnp.dot(wms[g], vb[cr, cs], preferred_element_type=F32) + bs_ref[:, g:g + 1]
                groups.append((u[cr, cs] * sv).astype(BF16))
            chunks.append(jnp.concatenate(groups, axis=1))
        gated = chunks[0] if len(chunks) == 1 else jnp.concatenate(chunks, axis=0)
        h = jnp.dot(gated, wout_ref[...], preferred_element_type=F32)
        y = _layer_norm(ALPHA * x_ref[rows, :] + h, g_ref[...], b_ref[...])
        y_ref[rows, :] = y
        yb_ref[rows, :] = y.astype(BF16)


def _gmlp(xb, x, O, g, b, *, T, tm, emit_v):
    m = x.shape[0]
    L = min(T, CHUNK_C)
    ws = O["ws"][:, :L, :L]
    bs_t = jnp.transpose(O["bs"][:, :L])
    row = pl.BlockSpec((tm, D_MODEL), lambda i: (i, 0))
    vec = pl.BlockSpec((1, D_MODEL), lambda i: (0, 0))
    return pl.pallas_call(
        functools.partial(_gmlp_kernel, tm=tm, L=L),
        grid=(m // tm,),
        in_specs=[row, row, _resident(O["w_in"].shape), vec, vec,
                  pl.BlockSpec(ws.shape, lambda i: (0, 0, 0)), pl.BlockSpec(bs_t.shape, lambda i: (0, 0)),
                  _resident(O["w_out"].shape), vec, vec],
        out_specs=[row] * (3 if emit_v else 2),
        out_shape=[jax.ShapeDtypeStruct((m, D_MODEL), F32), jax.ShapeDtypeStruct((m, D_MODEL), BF16)]
        + ([jax.ShapeDtypeStruct((m, DC), F32)] if emit_v else []),
        compiler_params=_cparams(1),
        name="gmlp",
    )(xb, x, O["w_in"], O["ln_v_g"].reshape(1, DC), O["ln_v_b"].reshape(1, DC), ws, bs_t,
      O["w_out"], g.reshape(1, D_MODEL), b.reshape(1, D_MODEL))


def _ffn_up_kernel(x_ref, w1_ref, w2_ref, hist_ref, cw_ref, cb_ref, h_ref, st_ref,
                   wb_ref, carry_ref, *, tm, seg, tiles_per_seq):
    j = pl.program_id(0)
    i = pl.program_id(1)
    tn = FF_TILE

    @pl.when(i == 0)
    def _():
        keep = j * tn + lax.broadcasted_iota(jnp.int32, (1, tn), 1) < D_FF
        wb_ref[:, :tn] = jnp.where(keep, w1_ref[...], 0.0).astype(BF16)
        wb_ref[:, tn:] = jnp.where(keep, w2_ref[...], 0.0).astype(BF16)
        carry_ref[...] = jnp.zeros_like(carry_ref)

    n_seg = tm // seg
    rg = min(tm, 256)
    piece = min(seg, rg)
    seq_start = i % tiles_per_seq == 0
    sub_i = lax.broadcasted_iota(jnp.int32, (SUBLANE, MXU_N), 0)
    for c in range(tn // MXU_N):
        cs = slice(c * MXU_N, (c + 1) * MXU_N)
        prev = None
        for r in range(tm // rg):
            x = x_ref[r * rg:(r + 1) * rg, :]
            u = jnp.dot(x, wb_ref[:, c * MXU_N:(c + 1) * MXU_N], preferred_element_type=F32)
            z = jnp.dot(x, wb_ref[:, tn + c * MXU_N:tn + (c + 1) * MXU_N], preferred_element_type=F32)
            for p in range(rg // piece):
                row0 = r * rg + p * piece
                s = row0 // seg
                if row0 % seg == 0:
                    if n_seg == 1:
                        prev = jnp.where(seq_start, hist_ref[0, :, cs], carry_ref[:, cs])
                    else:
                        prev = hist_ref[s, :, cs]
                up = u[p * piece:(p + 1) * piece]
                zp = z[p * piece:(p + 1) * piece]
                u1 = pltpu.roll(up, 1, 0)
                u2 = pltpu.roll(up, 2, 0)
                h1 = jnp.where(sub_i < 1, pltpu.roll(prev, 1, 0), u1[:SUBLANE])
                h2 = jnp.where(sub_i < 2, pltpu.roll(prev, 2, 0), u2[:SUBLANE])
                u1 = jnp.concatenate([h1, u1[SUBLANE:]], axis=0)
                u2 = jnp.concatenate([h2, u2[SUBLANE:]], axis=0)
                cv = u2 * cw_ref[0:1, cs] + u1 * cw_ref[1:2, cs] + up * cw_ref[2:3, cs]
                h_ref[row0:row0 + piece, cs] = (jax.nn.gelu(cv + cb_ref[:, cs]) * zp).astype(BF16)
                prev = up[piece - SUBLANE:]
                if (row0 + piece) % seg == 0:
                    st_ref[s, :, cs] = prev
                    if n_seg == 1:
                        carry_ref[:, cs] = prev


def _ffn_up(xb, w1_all, w2_all, layer, hist8, cw, cb, *, T, tm):
    m = xb.shape[0]
    tn = FF_TILE
    seg = min(T, tm)
    n_seg = tm // seg
    tps = T // seg
    wspec = pl.BlockSpec((None, D_MODEL, tn), lambda j, i: (layer, 0, j))
    return pl.pallas_call(
        functools.partial(_ffn_up_kernel, tm=tm, seg=seg, tiles_per_seq=tps),
        grid=(N_FF_TILES, m // tm),
        in_specs=[
            pl.BlockSpec((tm, D_MODEL), lambda j, i: (i, 0)),
            wspec, wspec,
            pl.BlockSpec((n_seg, SUBLANE, tn), lambda j, i: (i // tps, 0, j)),
            pl.BlockSpec((CONV_W, tn), lambda j, i: (0, j)),
            pl.BlockSpec((1, tn), lambda j, i: (0, j)),
        ],
        out_specs=[
            pl.BlockSpec((tm, tn), lambda j, i: (i, j)),
            pl.BlockSpec((n_seg, SUBLANE, tn), lambda j, i: (i, 0, j)),
        ],
        out_shape=[
            jax.ShapeDtypeStruct((m, D_FF_PAD), BF16),
            jax.ShapeDtypeStruct((m // seg, SUBLANE, D_FF_PAD), F32),
        ],
        scratch_shapes=[pltpu.VMEM((D_MODEL, 2 * tn), BF16), pltpu.VMEM((SUBLANE, tn), F32)],
        compiler_params=_cparams(2),
        name="ffn_up",
    )(xb, w1_all, w2_all, hist8, cw, cb)


def _ffn_down_kernel(h_ref, w_ref, res_ref, g_ref, b_ref, y_ref, yb_ref, *, tm):
    for r in range(tm // ROW_SUB):
        rows = slice(r * ROW_SUB, (r + 1) * ROW_SUB)
        acc = jnp.dot(h_ref[rows, :], w_ref[...], preferred_element_type=F32)
        y = _layer_norm(ALPHA * res_ref[rows, :] + acc, g_ref[...], b_ref[...])
        y_ref[rows, :] = y
        yb_ref[rows, :] = y.astype(BF16)


def _ffn_down(h, w3, res, g, b, *, tm):
    m, n = res.shape
    row = lambda width: pl.BlockSpec((tm, width), lambda i: (i, 0))
    vec = pl.BlockSpec((1, n), lambda i: (0, 0))
    return pl.pallas_call(
        functools.partial(_ffn_down_kernel, tm=tm),
        grid=(m // tm,),
        in_specs=[row(D_FF_PAD), _resident(w3.shape), row(n), vec, vec],
        out_specs=[row(n), row(n)],
        out_shape=[jax.ShapeDtypeStruct((m, n), F32), jax.ShapeDtypeStruct((m, n), BF16)],
        compiler_params=_cparams(1, vmem=VMEM_LIMIT_FFN_DOWN),
        name="ffn_down",
    )(h, w3, res, g.reshape(1, n), b.reshape(1, n))


def _cast_pad_rows_kernel(w_ref, o_ref, *, rows_valid, tr):
    k = pl.program_id(0)
    keep = k * tr + lax.broadcasted_iota(jnp.int32, (tr, 1), 0) < rows_valid
    o_ref[...] = jnp.where(keep, w_ref[...], 0.0).astype(BF16)


def _cast_pad_rows(w_all, layer, rows_pad, *, tr=FF_TILE):
    _, rows, n = w_all.shape
    return pl.pallas_call(
        functools.partial(_cast_pad_rows_kernel, rows_valid=rows, tr=tr),
        grid=(rows_pad // tr,),
        in_specs=[pl.BlockSpec((None, tr, n), lambda k: (layer, k, 0))],
        out_specs=pl.BlockSpec((tr, n), lambda k: (k, 0)),
        out_shape=jax.ShapeDtypeStruct((rows_pad, n), BF16),
        compiler_params=_cparams(1),
        name="cast_pad_rows",
    )(w_all)


def _split_w_in_kernel(w_ref, qkg_ref, v_ref, og_ref, qb_ref, kv_ref):
    hk, hv, hd = H_A * DK_A, H_A * DV_A, H_B * DH_B
    qkg_ref[:, :2 * hk] = w_ref[:, :2 * hk].astype(BF16)
    o = 2 * hk
    v_ref[...] = w_ref[:, o:o + hv].astype(BF16)
    o += hv
    g = w_ref[:, o:o + GATE_PAD]
    lane = lax.broadcasted_iota(jnp.int32, g.shape, 1)
    qkg_ref[:, 2 * hk:] = jnp.where(lane < GATE_RANK, g, 0.0).astype(BF16)
    o += GATE_RANK
    og_ref[...] = w_ref[:, o:o + hv].astype(BF16)
    o += hv
    qb_ref[...] = w_ref[:, o:o + hd].astype(BF16)
    o += hd
    kv_ref[...] = w_ref[:, o:o + 2 * hd].astype(BF16)


def _split_w_in(w_in, *, tr=256):
    k, n = w_in.shape
    hv, hd = H_A * DV_A, H_B * DH_B
    widths = [QKG_W, hv, hv, hd, 2 * hd]
    return pl.pallas_call(
        _split_w_in_kernel,
        grid=(k // tr,),
        in_specs=[pl.BlockSpec((tr, n), lambda i: (i, 0))],
        out_specs=[pl.BlockSpec((tr, w), lambda i: (i, 0)) for w in widths],
        out_shape=[jax.ShapeDtypeStruct((k, w), BF16) for w in widths],
        compiler_params=_cparams(1),
        name="split_w_in",
    )(w_in)


def _prep_even(w_in, w_gate_up, b_gate, w_out):
    hk, hv = H_A * DK_A, H_A * DV_A
    w_qkg, w_v, w_og, w_qb, w_kv = _split_w_in(w_in)
    wup = jnp.pad(w_gate_up, ((0, GATE_PAD - GATE_RANK), (0, 0)))
    return dict(
        w_qkg=w_qkg, w_v=w_v, w_og=w_og, w_qb=w_qb, w_kv=w_kv,
        wup=wup.astype(BF16), bg=b_gate.reshape(1, hk),
        w_out_a=w_out[:hv].astype(BF16), w_out_b=w_out[hv:].astype(BF16),
    )


def _prep_ffn(cw, cb, w3_all, layer):
    pad = D_FF_PAD - D_FF
    return dict(layer=layer,
                cw=jnp.pad(cw, ((0, 0), (0, pad))),
                cb=jnp.pad(cb, (0, pad)).reshape(1, D_FF_PAD),
                w3=_cast_pad_rows(w3_all, layer, D_FF_PAD))


def _toeplitz_kernel(ext_ref, o_ref):
    heads, rows, width = o_ref.shape
    for h in range(heads):
        o_ref[h] = pltpu.roll(jnp.broadcast_to(ext_ref[h], (rows, width)), 0, 1, stride=1, stride_axis=0)


def _rel_toeplitz(table, n_rows, n_cols, offset):
    heads = table.shape[0]
    pad = -(-n_rows // LANE) * LANE
    width = -(-(pad + n_cols) // LANE) * LANE
    n_hi = offset + pad - REL_CLIP
    assert n_hi >= 0
    ext = jnp.concatenate([
        jnp.broadcast_to(table[:, -1:], (heads, n_hi)), table[:, ::-1],
        jnp.broadcast_to(table[:, :1], (heads, width))], axis=1)[:, :width]
    toe = pl.pallas_call(
        _toeplitz_kernel,
        grid=(1,),
        in_specs=[pl.BlockSpec((heads, 1, width), lambda i: (0, 0, 0))],
        out_specs=pl.BlockSpec((heads, n_rows, width), lambda i: (0, 0, 0)),
        out_shape=jax.ShapeDtypeStruct((heads, n_rows, width), F32),
        compiler_params=_cparams(1),
        name="rel_toeplitz",
    )(ext.astype(F32).reshape(heads, 1, width))
    return toe, pad


def _prompt_bias(table):
    n_special = BAND_PAST // ATT_QB
    toe, pad = _rel_toeplitz(table, ATT_QB, ATT_KW + BAND_PAST, BAND_PAST)
    r = jnp.arange(ATT_QB)[:, None]
    j = jnp.arange(ATT_KW)[None, :]
    out = []
    for qb in range(n_special + 1):
        q_chunk = (qb * ATT_QB + r) // CHUNK
        valid = (j >= q_chunk * CHUNK - BAND_PAST) & (j < (q_chunk + 1) * CHUNK)
        shift = pad + (n_special - qb) * ATT_QB
        out.append(jnp.where(valid[None], toe[:, :, shift:shift + ATT_KW] * LOG2E, NEG_INF))
    return jnp.stack(out)


def _step_bias(table, T, lc):
    toe, pad = _rel_toeplitz(table, T, lc + T, lc)
    return toe[:, :, pad:pad + lc], toe[:, :, pad + lc:pad + lc + T]


def _conv_ffn(x, xb, F, ln_g, ln_b, hist, *, B, T, tm_up, tm_down):
    pad = D_FF_PAD - D_FF
    keep = CONV_W - 1
    hist8 = jnp.pad(hist, ((0, 0), (SUBLANE - keep, 0), (0, pad)))
    h, st8 = _ffn_up(xb, F["w1_all"], F["w2_all"], F["layer"], hist8, F["cw"], F["cb"], T=T, tm=tm_up)
    new_state = st8.reshape(B, -1, SUBLANE, D_FF_PAD)[:, -1, SUBLANE - keep:, :D_FF]
    y, yb = _ffn_down(h, F["w3"], x, ln_g, ln_b, tm=tm_down)
    return y, yb, new_state


def _trunk(x3, E, Fs, O, norms, state, bias):
    B, T, _ = x3.shape
    m = B * T
    tm = min(512, m)
    tm_lin = min(1024, m)
    tm_up = min(2048, m)
    tm_small = min(256, m)
    x = x3.reshape(m, D_MODEL)
    hv, hd = H_A * DV_A, H_B * DH_B
    prompt = state is None

    qkg, xb = _linear(x, E["w_qkg"], out_dtype=F32, tm=tm_lin, emit_xb=True)
    v_a = _linear(xb, E["w_v"], out_dtype=BF16, tm=tm_lin)
    og = _linear(xb, E["w_og"], out_dtype=F32, tm=tm_lin)
    q_b = _linear(xb, E["w_qb"], out_dtype=BF16, tm=tm_lin)
    if prompt:
        s0_t = jnp.zeros((B, H_A, DV_A, DK_A), F32)
        conv_hist = [jnp.zeros((B, CONV_W - 1, D_FF), F32)] * DEPTH
    else:
        cache_k, cache_v, state_gla, state_conv = state
        s0_t = jnp.swapaxes(state_gla[0], -1, -2)
        conv_hist = [state_conv[i] for i in range(DEPTH)]
    o_a, s_fin_t = _gla(qkg, v_a, og, E["wup"], E["bg"], s0_t, E["gn"], B=B, T=T)
    new_gla = jnp.swapaxes(s_fin_t, -1, -2)[None]
    if prompt:
        kv_b = _linear(xb, E["w_kv"], out_dtype=BF16, tm=tm_lin)
        o_b = _band_attn(q_b, kv_b, bias, B=B, T=T)
        keep = min(BAND_PAST, T)
        blocks_per_seq = T // keep
        kv_tail = _linear(xb, E["w_kv"], out_dtype=F32, tm=keep, n_row_blocks=B,
                          x_block_index=lambda i: i * blocks_per_seq + blocks_per_seq - 1)
        kv_rows = kv_tail.reshape(B, keep, 2, H_B, DH_B)
    else:
        kv_f = _linear(xb, E["w_kv"], out_dtype=F32, tm=tm_lin)
        o_b = _attn_step(q_b, kv_f, cache_k, cache_v, bias[0], bias[1], B=B, T=T)
        kv_rows = kv_f.reshape(B, T, 2, H_B, DH_B)
    new_k = kv_rows[:, :, 0][None]
    new_v = kv_rows[:, :, 1][None]
    x, xb = _proj_ln(o_a, o_b, E["w_out_a"], E["w_out_b"], x, norms["ln1_g"][0], norms["ln1_b"][0], tm=tm)
    x, xb, conv0 = _conv_ffn(x, xb, Fs[0], norms["ln2_g"][0], norms["ln2_b"][0], conv_hist[0],
                             B=B, T=T, tm_up=tm_up, tm_down=tm)

    x, xb, *maybe_v = _gmlp(xb, x, O, norms["ln1_g"][1], norms["ln1_b"][1], T=T,
                            tm=tm_small if not prompt else tm, emit_v=not prompt)
    x, xb, conv1 = _conv_ffn(x, xb, Fs[1], norms["ln2_g"][1], norms["ln2_b"][1], conv_hist[1],
                             B=B, T=T, tm_up=tm_up, tm_down=tm)

    y = x.reshape(B, T, D_MODEL)
    new_conv = jnp.stack([conv0, conv1])
    mlp_v = maybe_v[0].reshape(B, T, DC)[None] if maybe_v else None
    return y, new_k, new_v, new_gla, new_conv, mlp_v


def kernel(x_prompt, x_sample, cache_attn_k, cache_attn_v, state_gla, state_ffn_conv, w_in_even, w_gate_up, b_gate, gla_norm_g, rel_bias, w_out_even, w_in_odd, ln_v_g, ln_v_b, w_spatial, b_spatial, w_out_odd, ffn_w1, ffn_w2, ffn_conv_w, ffn_conv_b, ffn_w3, ln1_g, ln1_b, ln2_g, ln2_b):
    E = _prep_even(w_in_even.reshape(w_in_even.shape[-2:]), w_gate_up[0], b_gate[0], w_out_even[0])
    E["gn"] = gla_norm_g[0]
    Fs = [dict(_prep_ffn(ffn_conv_w[i], ffn_conv_b[i], ffn_w3, i), w1_all=ffn_w1, w2_all=ffn_w2)
          for i in range(DEPTH)]
    O = dict(w_in=w_in_odd[0].astype(BF16), ln_v_g=ln_v_g[0], ln_v_b=ln_v_b[0],
             ws=w_spatial[0], bs=b_spatial[0], w_out=w_out_odd[0].astype(BF16))
    norms = dict(ln1_g=ln1_g, ln1_b=ln1_b, ln2_g=ln2_g, ln2_b=ln2_b)

    bias_p = _prompt_bias(rel_bias[0])
    y_p, k_p, v_p, gla_p, conv_p, _ = _trunk(x_prompt, E, Fs, O, norms, None, bias_p)

    bias_s = _step_bias(rel_bias[0], x_sample.shape[1], cache_attn_k.shape[2])
    y_s, k_s, v_s, gla_s, conv_s, mlp_v_s = _trunk(
        x_sample, E, Fs, O, norms, (cache_attn_k, cache_attn_v, state_gla, state_ffn_conv), bias_s)
    return (y_p, y_s, k_p, v_p, gla_p, conv_p, k_s, v_s, gla_s, conv_s, mlp_v_s)
```

```python
import functools

import jax
import jax.numpy as jnp
from jax import lax
from jax.experimental import pallas as pl
from jax.experimental.pallas import tpu as pltpu

F32 = jnp.float32
BF16 = jnp.bfloat16

D_MODEL = 2048
DEPTH = 2
CHUNK = 64
H_A = 4
DK_A = D_MODEL // 16
DV_A = D_MODEL // 8
GATE_RANK = 16
GATE_TEMP = 16.0
H_B = 8
DH_B = D_MODEL // 16
N_PREV_CHUNKS = 8
BAND_PAST = N_PREV_CHUNKS * CHUNK
REL_CLIP = 128
CHUNK_C = 128
DC = D_MODEL
G_C = 8
D_FF = ((8 * D_MODEL // 3 + 127) // 128) * 128
CONV_W = 3
ALPHA = (2 * DEPTH) ** 0.25
LN_EPS = 1e-5
NEG_INF = -1e30
PAST_LEN = 2048
LOG2E = 1.4426950408889634

LANE = 128
SUBLANE = 8
MXU_N = 256
FF_TILE = 512
D_FF_PAD = ((D_FF + FF_TILE - 1) // FF_TILE) * FF_TILE
N_FF_TILES = D_FF_PAD // FF_TILE
GATE_PAD = LANE
QKG_W = 2 * H_A * DK_A + GATE_PAD
ATT_QB = 4 * CHUNK
ATT_KW = BAND_PAST + ATT_QB
ROW_SUB = 128
VMEM_LIMIT = 56 * 1024 * 1024
VMEM_LIMIT_FFN_DOWN = 58 * 1024 * 1024


def _cparams(n_axes, vmem=VMEM_LIMIT, **kw):
    return pltpu.CompilerParams(dimension_semantics=("arbitrary",) * n_axes,
                                vmem_limit_bytes=vmem, **kw)


def _resident(shape):
    nd = len(shape)
    return pl.BlockSpec(shape, lambda *_: (0,) * nd, pipeline_mode=pl.Buffered(1))


def _layer_norm(xf, g, b):
    mu = jnp.mean(xf, -1, keepdims=True)
    xc = xf - mu
    var = jnp.mean(xc * xc, -1, keepdims=True)
    y = xc * lax.rsqrt(var + LN_EPS)
    return y * g + b


def _log_sigmoid(x):
    return jnp.minimum(x, 0.0) - jnp.log1p(jnp.exp(-jnp.abs(x)))


def _linear_kernel(x_ref, w_ref, o_ref, *extra_refs, emit_xb, tail):
    xb = x_ref[...].astype(BF16)
    y = jnp.dot(xb, w_ref[...], preferred_element_type=F32)
    o_ref[...] = y.astype(o_ref.dtype)
    if emit_xb:
        extra_refs[0][...] = xb
    if tail:
        extra_refs[-1][...] = y[y.shape[0] - tail:, :]


def _linear(x, w, *, out_dtype, tm, emit_xb=False, tail=None):
    m, k = x.shape
    n = w.shape[1]
    out_shape = [jax.ShapeDtypeStruct((m, n), out_dtype)]
    out_specs = [pl.BlockSpec((tm, n), lambda i: (i, 0))]
    if emit_xb:
        out_shape.append(jax.ShapeDtypeStruct((m, k), BF16))
        out_specs.append(pl.BlockSpec((tm, k), lambda i: (i, 0)))
    keep = 0
    if tail is not None:
        tps, keep = tail
        assert keep <= tm
        out_shape.append(jax.ShapeDtypeStruct((m // (tm * tps) * keep, n), F32))
        out_specs.append(pl.BlockSpec((keep, n), lambda i: (i // tps, 0)))
    res = pl.pallas_call(
        functools.partial(_linear_kernel, emit_xb=emit_xb, tail=keep),
        grid=(m // tm,),
        in_specs=[pl.BlockSpec((tm, k), lambda i: (i, 0)), _resident(w.shape)],
        out_specs=out_specs,
        out_shape=out_shape,
        compiler_params=_cparams(1),
        name="linear",
    )(x, w)
    return res if len(res) > 1 else res[0]


def _proj_ln_kernel(xa_ref, xb_ref, wa_ref, wb_ref, res_ref, g_ref, b_ref, y_ref, yb_ref, *, tm):
    for r in range(tm // ROW_SUB):
        rows = slice(r * ROW_SUB, (r + 1) * ROW_SUB)
        h = (jnp.dot(xa_ref[rows, :], wa_ref[...], preferred_element_type=F32)
             + jnp.dot(xb_ref[rows, :], wb_ref[...], preferred_element_type=F32))
        y = _layer_norm(ALPHA * res_ref[rows, :] + h, g_ref[...], b_ref[...])
        y_ref[rows, :] = y
        yb_ref[rows, :] = y.astype(BF16)


def _proj_ln(xa, xb, wa, wb, res, g, b, *, tm):
    m, n = res.shape
    row = lambda width: pl.BlockSpec((tm, width), lambda i: (i, 0))
    vec = pl.BlockSpec((1, n), lambda i: (0, 0))
    return pl.pallas_call(
        functools.partial(_proj_ln_kernel, tm=tm),
        grid=(m // tm,),
        in_specs=[row(xa.shape[1]), row(xb.shape[1]), _resident(wa.shape), _resident(wb.shape),
                  row(n), vec, vec],
        out_specs=[row(n), row(n)],
        out_shape=[jax.ShapeDtypeStruct((m, n), F32), jax.ShapeDtypeStruct((m, n), BF16)],
        compiler_params=_cparams(1),
        name="proj_ln",
    )(xa, xb, wa, wb, res, g.reshape(1, n), b.reshape(1, n))


def _gla_kernel(qkg_ref, v_ref, og_ref, wup_ref, bg_ref, s0_ref, gn_ref,
                o_ref, sfin_ref, st_ref, *, L, n_steps, bb, cps):
    c = pl.program_id(1)

    @pl.when(c == 0)
    def _():
        st_ref[...] = s0_ref[...]

    hk = H_A * DK_A
    row = lax.broadcasted_iota(jnp.int32, (L, L), 0)
    col = lax.broadcasted_iota(jnp.int32, (L, L), 1)
    causal = row >= col
    tri = jnp.where(causal, 1.0, 0.0).astype(BF16)
    nt = (((1,), (1,)), ((), ()))
    tn_dims = (((0,), (0,)), ((), ()))

    chains = [(ci, b, h) for ci in range(cps) for b in range(bb) for h in range(H_A)]
    rows = lambda ci: slice(ci * L, (ci + 1) * L)
    bcum = {}
    for ci in range(cps):
        for b in range(bb):
            glow = qkg_ref[b, rows(ci), 2 * hk:2 * hk + GATE_PAD].astype(BF16)
            gate = jnp.dot(glow, wup_ref[...], preferred_element_type=F32) + bg_ref[...]
            logg = _log_sigmoid(gate) * (1.0 / GATE_TEMP)
            hi = logg.astype(BF16)
            r1 = logg - hi.astype(F32)
            mid = r1.astype(BF16)
            lo = (r1 - mid.astype(F32)).astype(BF16)
            bcum[ci, b] = (jnp.dot(tri, hi, preferred_element_type=F32)
                           + jnp.dot(tri, mid, preferred_element_type=F32)
                           + jnp.dot(tri, lo, preferred_element_type=F32))
    q_dec, k_dec, k_upd, decay = {}, {}, {}, {}
    for (ci, b, h) in chains:
        bh = bcum[ci, b][:, h * DK_A:(h + 1) * DK_A]
        qh = qkg_ref[b, rows(ci), h * DK_A:(h + 1) * DK_A]
        kh = qkg_ref[b, rows(ci), hk + h * DK_A:hk + (h + 1) * DK_A]
        b_last = bh[L - 1:L, :]
        q_dec[ci, b, h] = ((qh * (DK_A ** -0.5)) * jnp.exp(bh)).astype(BF16)
        k_dec[ci, b, h] = (kh * jnp.exp(-bh)).astype(BF16)
        k_upd[ci, b, h] = (kh * jnp.exp(b_last - bh)).astype(BF16)
        decay[ci, b, h] = jnp.exp(b_last)
    att = {}
    for key in chains:
        a = lax.dot_general(q_dec[key], k_dec[key], nt, preferred_element_type=F32)
        att[key] = jnp.where(causal, a, 0.0).astype(BF16)
    outs = {}
    for (ci, b, h) in chains:
        key = (ci, b, h)
        vh = v_ref[b, rows(ci), h * DV_A:(h + 1) * DV_A]
        st = st_ref[b, h]
        outs[key] = (lax.dot_general(q_dec[key], st.astype(BF16), nt, preferred_element_type=F32)
                     + jnp.dot(att[key], vh, preferred_element_type=F32))
        st_ref[b, h] = st * decay[key] + lax.dot_general(
            vh, k_upd[key], tn_dims, preferred_element_type=F32)
    for (ci, b, h) in chains:
        vs = slice(h * DV_A, (h + 1) * DV_A)
        o = outs[ci, b, h]
        on = o * lax.rsqrt(jnp.mean(o * o, -1, keepdims=True) + LN_EPS) * gn_ref[h:h + 1, :]
        o_ref[b, rows(ci), vs] = (on * jax.nn.silu(og_ref[b, rows(ci), vs])).astype(BF16)

    @pl.when(c == n_steps - 1)
    def _():
        sfin_ref[...] = st_ref[...]


def _gla(qkg, v_a, og, wup, bg, s0_t, gn, *, B, T):
    L = min(T, CHUNK)
    n = T // L
    cps = 2 if n % 2 == 0 else 1
    bb = min(B, 4)
    hv = H_A * DV_A
    seq = lambda width: pl.BlockSpec((bb, cps * L, width), lambda g, c: (g, c, 0))
    const2 = lambda g, c: (0, 0)
    state = pl.BlockSpec((bb, H_A, DV_A, DK_A), lambda g, c: (g, 0, 0, 0))
    o, s_fin = pl.pallas_call(
        functools.partial(_gla_kernel, L=L, n_steps=n // cps, bb=bb, cps=cps),
        grid=(B // bb, n // cps),
        in_specs=[seq(QKG_W), seq(hv), seq(hv),
                  pl.BlockSpec(wup.shape, const2), pl.BlockSpec(bg.shape, const2),
                  state, pl.BlockSpec(gn.shape, const2)],
        out_specs=[seq(hv), state],
        out_shape=[jax.ShapeDtypeStruct((B, T, hv), BF16),
                   jax.ShapeDtypeStruct((B, H_A, DV_A, DK_A), F32)],
        scratch_shapes=[pltpu.VMEM((bb, H_A, DV_A, DK_A), F32)],
        compiler_params=_cparams(2),
        name="gla",
    )(qkg.reshape(B, T, QKG_W), v_a.reshape(B, T, hv), og.reshape(B, T, hv), wup, bg, s0_t, gn)
    return o.reshape(B * T, hv), s_fin


def _band_attn_kernel(q_ref, kv_ref, bias_ref, o_ref):
    qb = pl.program_id(1)
    start = pl.multiple_of(jnp.maximum(qb * ATT_QB - BAND_PAST, 0), ATT_QB)
    hd = H_B * DH_B
    nt = (((1,), (1,)), ((), ()))

    def logits2(h):
        q = q_ref[:, h * DH_B:(h + 1) * DH_B]
        k = kv_ref[pl.ds(start, ATT_KW), h * DH_B:(h + 1) * DH_B]
        return (lax.dot_general(q, k, nt, preferred_element_type=F32) * (DH_B ** -0.5 * LOG2E)
                + bias_ref[0, h])

    def weights(t):
        e = jnp.exp2(t - jnp.max(t, -1, keepdims=True))
        return e.astype(BF16), jnp.sum(e, -1, keepdims=True)

    def output(h, e, den):
        v = kv_ref[pl.ds(start, ATT_KW), hd + h * DH_B:hd + (h + 1) * DH_B]
        o = jnp.dot(e, v, preferred_element_type=F32) / den
        o_ref[:, h * DH_B:(h + 1) * DH_B] = o.astype(BF16)

    t_next = logits2(0)
    for h in range(H_B):
        t_cur = t_next
        if h + 1 < H_B:
            t_next = logits2(h + 1)
        output(h, *weights(t_cur))


def _band_attn(q_b, kv_b, bias_blocks, *, B, T):
    assert T % ATT_QB == 0 and T >= ATT_KW
    n = T // ATT_QB
    hd = H_B * DH_B
    last_bias = bias_blocks.shape[0] - 1
    return pl.pallas_call(
        _band_attn_kernel,
        grid=(B, n),
        in_specs=[
            pl.BlockSpec((ATT_QB, hd), lambda b, c: (b * n + c, 0)),
            pl.BlockSpec((T, 2 * hd), lambda b, c: (b, 0)),
            pl.BlockSpec((1, H_B, ATT_QB, ATT_KW), lambda b, c: (jnp.minimum(c, last_bias), 0, 0, 0)),
        ],
        out_specs=pl.BlockSpec((ATT_QB, hd), lambda b, c: (b * n + c, 0)),
        out_shape=jax.ShapeDtypeStruct((B * T, hd), BF16),
        compiler_params=_cparams(2),
        name="band_attn",
    )(q_b, kv_b, bias_blocks)


def _attn_step_kernel(q_ref, kv_ref, ck_ref, cv_ref, bias_c_ref, bias_n_ref, o_ref):
    hd = H_B * DH_B
    lc = ck_ref.shape[0] // H_B
    nt = (((1,), (1,)), ((), ()))
    scale = DH_B ** -0.5
    def scores(h):
        hs = slice(h * DH_B, (h + 1) * DH_B)
        q = q_ref[:, hs].astype(BF16)
        kc = ck_ref[pl.ds(h, lc, stride=H_B), :].astype(BF16)
        kn = kv_ref[:, hs].astype(BF16)
        sc = lax.dot_general(q, kc, nt, preferred_element_type=F32) * scale + bias_c_ref[h]
        sn = lax.dot_general(q, kn, nt, preferred_element_type=F32) * scale + bias_n_ref[h]
        return sc, sn

    def output(h, sc, sn):
        vc = cv_ref[pl.ds(h, lc, stride=H_B), :].astype(BF16)
        vn = kv_ref[:, hd + h * DH_B:hd + (h + 1) * DH_B].astype(BF16)
        m = jnp.maximum(jnp.max(sc, -1, keepdims=True), jnp.max(sn, -1, keepdims=True))
        ec = jnp.exp(sc - m)
        en = jnp.exp(sn - m)
        den = jnp.sum(ec, -1, keepdims=True) + jnp.sum(en, -1, keepdims=True)
        o = (jnp.dot((ec / den).astype(BF16), vc, preferred_element_type=F32)
             + jnp.dot((en / den).astype(BF16), vn, preferred_element_type=F32))
        o_ref[:, h * DH_B:(h + 1) * DH_B] = o.astype(BF16)

    s_next = scores(0)
    for h in range(H_B):
        s_cur = s_next
        if h + 1 < H_B:
            s_next = scores(h + 1)
        output(h, *s_cur)


def _attn_step(q_b, kv_b, cache_k, cache_v, bias_c, bias_n, *, B, T):
    hd = H_B * DH_B
    lc = cache_k.shape[2]
    ck = cache_k.reshape(1, B, lc * H_B, DH_B)
    cv = cache_v.reshape(1, B, lc * H_B, DH_B)
    cache = pl.BlockSpec((None, None, lc * H_B, DH_B), lambda b: (0, b, 0, 0))
    return pl.pallas_call(
        _attn_step_kernel,
        grid=(B,),
        in_specs=[
            pl.BlockSpec((T, hd), lambda b: (b, 0)),
            pl.BlockSpec((T, 2 * hd), lambda b: (b, 0)),
            cache, cache,
            pl.BlockSpec(bias_c.shape, lambda b: (0, 0, 0)),
            pl.BlockSpec(bias_n.shape, lambda b: (0, 0, 0)),
        ],
        out_specs=pl.BlockSpec((T, hd), lambda b: (b, 0)),
        out_shape=jax.ShapeDtypeStruct((B * T, hd), BF16),
        compiler_params=_cparams(1),
        name="attn_step",
    )(q_b, kv_b, ck, cv, bias_c, bias_n)


def _gmlp_kernel(xb_ref, x_ref, win_ref, lvg_ref, lvb_ref, ws_ref, bs_ref, wout_ref, g_ref, b_ref,
                 y_ref, yb_ref, *maybe_v_ref, tm, L):
    row = lax.broadcasted_iota(jnp.int32, (L, L), 0)
    col = lax.broadcasted_iota(jnp.int32, (L, L), 1)
    causal = row >= col
    gw = DC // G_C
    wms = [jnp.where(causal, ws_ref[g], 0.0).astype(BF16) for g in range(G_C)]
    sub = max(ROW_SUB, L)
    for r in range(tm // sub):
        rows = slice(r * sub, (r + 1) * sub)
        uv = jax.nn.gelu(jnp.dot(xb_ref[rows, :], win_ref[...], preferred_element_type=F32))
        u = uv[:, :DC]
        v = _layer_norm(uv[:, DC:], lvg_ref[...], lvb_ref[...])
        if maybe_v_ref:
            maybe_v_ref[0][rows, :] = v
        vb = v.astype(BF16)
        chunks = []
        for c in range(sub // L):
            cr = slice(c * L, (c + 1) * L)
            groups = []
            for g in range(G_C):
                cs = slice(g * gw, (g + 1) * gw)
                sv = jnp.dot(wms[g], vb[cr, cs], preferred_element_type=F32) + bs_ref[:, g:g + 1]
                groups.append((u[cr, cs] * sv).astype(BF16))
            chunks.append(jnp.concatenate(groups, axis=1))
        gated = chunks[0] if len(chunks) == 1 else jnp.concatenate(chunks, axis=0)
        h = jnp.dot(gated, wout_ref[...], preferred_element_type=F32)
        y = _layer_norm(ALPHA * x_ref[rows, :] + h, g_ref[...], b_ref[...])
        y_ref[rows, :] = y
        yb_ref[rows, :] = y.astype(BF16)


def _gmlp(xb, x, O, g, b, *, T, tm, emit_v):
    m = x.shape[0]
    L = min(T, CHUNK_C)
    ws = O["ws"][:, :L, :L]
    bs_t = jnp.transpose(O["bs"][:, :L])
    row = pl.BlockSpec((tm, D_MODEL), lambda i: (i, 0))
    vec = pl.BlockSpec((1, D_MODEL), lambda i: (0, 0))
    return pl.pallas_call(
        functools.partial(_gmlp_kernel, tm=tm, L=L),
        grid=(m // tm,),
        in_specs=[row, row, _resident(O["w_in"].shape), vec, vec,
                  pl.BlockSpec(ws.shape, lambda i: (0, 0, 0)), pl.BlockSpec(bs_t.shape, lambda i: (0, 0)),
                  _resident(O["w_out"].shape), vec, vec],
        out_specs=[row] * (3 if emit_v else 2),
        out_shape=[jax.ShapeDtypeStruct((m, D_MODEL), F32), jax.ShapeDtypeStruct((m, D_MODEL), BF16)]
        + ([jax.ShapeDtypeStruct((m, DC), F32)] if emit_v else []),
        compiler_params=_cparams(1),
        name="gmlp",
    )(xb, x, O["w_in"], O["ln_v_g"].reshape(1, DC), O["ln_v_b"].reshape(1, DC), ws, bs_t,
      O["w_out"], g.reshape(1, D_MODEL), b.reshape(1, D_MODEL))


def _ffn_up_kernel(x_ref, w1_ref, w2_ref, hist_ref, cw_ref, cb_ref, h_ref, st_ref,
                   wb_ref, carry_ref, *, tm, seg, tiles_per_seq):
    j = pl.program_id(0)
    i = pl.program_id(1)
    tn = FF_TILE

    @pl.when(i == 0)
    def _():
        keep = j * tn + lax.broadcasted_iota(jnp.int32, (1, tn), 1) < D_FF
        wb_ref[:, :tn] = jnp.where(keep, w1_ref[...], 0.0).astype(BF16)
        wb_ref[:, tn:] = jnp.where(keep, w2_ref[...], 0.0).astype(BF16)
        carry_ref[...] = jnp.zeros_like(carry_ref)

    n_seg = tm // seg
    rg = min(tm, 256)
    piece = min(seg, rg)
    seq_start = i % tiles_per_seq == 0
    sub_i = lax.broadcasted_iota(jnp.int32, (SUBLANE, MXU_N), 0)
    for c in range(tn // MXU_N):
        cs = slice(c * MXU_N, (c + 1) * MXU_N)
        prev = None
        for r in range(tm // rg):
            x = x_ref[r * rg:(r + 1) * rg, :]
            u = jnp.dot(x, wb_ref[:, c * MXU_N:(c + 1) * MXU_N], preferred_element_type=F32)
            z = jnp.dot(x, wb_ref[:, tn + c * MXU_N:tn + (c + 1) * MXU_N], preferred_element_type=F32)
            for p in range(rg // piece):
                row0 = r * rg + p * piece
                s = row0 // seg
                if row0 % seg == 0:
                    if n_seg == 1:
                        prev = jnp.where(seq_start, hist_ref[0, :, cs], carry_ref[:, cs])
                    else:
                        prev = hist_ref[s, :, cs]
                up = u[p * piece:(p + 1) * piece]
                zp = z[p * piece:(p + 1) * piece]
                u1 = pltpu.roll(up, 1, 0)
                u2 = pltpu.roll(up, 2, 0)
                h1 = jnp.where(sub_i < 1, pltpu.roll(prev, 1, 0), u1[:SUBLANE])
                h2 = jnp.where(sub_i < 2, pltpu.roll(prev, 2, 0), u2[:SUBLANE])
                u1 = jnp.concatenate([h1, u1[SUBLANE:]], axis=0)
                u2 = jnp.concatenate([h2, u2[SUBLANE:]], axis=0)
                cv = u2 * cw_ref[0:1, cs] + u1 * cw_ref[1:2, cs] + up * cw_ref[2:3, cs]
                h_ref[row0:row0 + piece, cs] = (jax.nn.gelu(cv + cb_ref[:, cs]) * zp).astype(BF16)
                prev = up[piece - SUBLANE:]
                if (row0 + piece) % seg == 0:
                    st_ref[s, :, cs] = prev
                    if n_seg == 1:
                        carry_ref[:, cs] = prev


def _ffn_up(xb, w1_all, w2_all, layer, hist8, cw, cb, *, T, tm):
    m = xb.shape[0]
    tn = FF_TILE
    seg = min(T, tm)
    n_seg = tm // seg
    tps = T // seg
    wspec = pl.BlockSpec((None, D_MODEL, tn), lambda j, i: (layer, 0, j))
    return pl.pallas_call(
        functools.partial(_ffn_up_kernel, tm=tm, seg=seg, tiles_per_seq=tps),
        grid=(N_FF_TILES, m // tm),
        in_specs=[
            pl.BlockSpec((tm, D_MODEL), lambda j, i: (i, 0)),
            wspec, wspec,
            pl.BlockSpec((n_seg, SUBLANE, tn), lambda j, i: (i // tps, 0, j)),
            pl.BlockSpec((CONV_W, tn), lambda j, i: (0, j)),
            pl.BlockSpec((1, tn), lambda j, i: (0, j)),
        ],
        out_specs=[
            pl.BlockSpec((tm, tn), lambda j, i: (i, j)),
            pl.BlockSpec((n_seg, SUBLANE, tn), lambda j, i: (i, 0, j)),
        ],
        out_shape=[
            jax.ShapeDtypeStruct((m, D_FF_PAD), BF16),
            jax.ShapeDtypeStruct((m // seg, SUBLANE, D_FF_PAD), F32),
        ],
        scratch_shapes=[pltpu.VMEM((D_MODEL, 2 * tn), BF16), pltpu.VMEM((SUBLANE, tn), F32)],
        compiler_params=_cparams(2),
        name="ffn_up",
    )(xb, w1_all, w2_all, hist8, cw, cb)


def _ffn_down_kernel(h_ref, w_ref, res_ref, g_ref, b_ref, y_ref, yb_ref, *, tm):
    for r in range(tm // ROW_SUB):
        rows = slice(r * ROW_SUB, (r + 1) * ROW_SUB)
        acc = jnp.dot(h_ref[rows, :], w_ref[...], preferred_element_type=F32)
        y = _layer_norm(ALPHA * res_ref[rows, :] + acc, g_ref[...], b_ref[...])
        y_ref[rows, :] = y
        yb_ref[rows, :] = y.astype(BF16)


def _ffn_down(h, w3, res, g, b, *, tm):
    m, n = res.shape
    row = lambda width: pl.BlockSpec((tm, width), lambda i: (i, 0))
    vec = pl.BlockSpec((1, n), lambda i: (0, 0))
    return pl.pallas_call(
        functools.partial(_ffn_down_kernel, tm=tm),
        grid=(m // tm,),
        in_specs=[row(D_FF_PAD), _resident(w3.shape), row(n), vec, vec],
        out_specs=[row(n), row(n)],
        out_shape=[jax.ShapeDtypeStruct((m, n), F32), jax.ShapeDtypeStruct((m, n), BF16)],
        compiler_params=_cparams(1, vmem=VMEM_LIMIT_FFN_DOWN),
        name="ffn_down",
    )(h, w3, res, g.reshape(1, n), b.reshape(1, n))


def _cast_pad_rows_kernel(w_ref, o_ref, *, rows_valid, tr):
    k = pl.program_id(0)
    keep = k * tr + lax.broadcasted_iota(jnp.int32, (tr, 1), 0) < rows_valid
    o_ref[...] = jnp.where(keep, w_ref[...], 0.0).astype(BF16)


def _cast_pad_rows(w_all, layer, rows_pad, *, tr=FF_TILE):
    _, rows, n = w_all.shape
    return pl.pallas_call(
        functools.partial(_cast_pad_rows_kernel, rows_valid=rows, tr=tr),
        grid=(rows_pad // tr,),
        in_specs=[pl.BlockSpec((None, tr, n), lambda k: (layer, k, 0))],
        out_specs=pl.BlockSpec((tr, n), lambda k: (k, 0)),
        out_shape=jax.ShapeDtypeStruct((rows_pad, n), BF16),
        compiler_params=_cparams(1),
        name="cast_pad_rows",
    )(w_all)


def _split_w_in_kernel(w_ref, qkg_ref, v_ref, og_ref, qb_ref, kv_ref):
    hk, hv, hd = H_A * DK_A, H_A * DV_A, H_B * DH_B
    qkg_ref[:, :2 * hk] = w_ref[:, :2 * hk].astype(BF16)
    o = 2 * hk
    v_ref[...] = w_ref[:, o:o + hv].astype(BF16)
    o += hv
    g = w_ref[:, o:o + GATE_PAD]
    lane = lax.broadcasted_iota(jnp.int32, g.shape, 1)
    qkg_ref[:, 2 * hk:] = jnp.where(lane < GATE_RANK, g, 0.0).astype(BF16)
    o += GATE_RANK
    og_ref[...] = w_ref[:, o:o + hv].astype(BF16)
    o += hv
    qb_ref[...] = w_ref[:, o:o + hd].astype(BF16)
    o += hd
    kv_ref[...] = w_ref[:, o:o + 2 * hd].astype(BF16)


def _split_w_in(w_in, *, tr=256):
    k, n = w_in.shape
    hv, hd = H_A * DV_A, H_B * DH_B
    widths = [QKG_W, hv, hv, hd, 2 * hd]
    return pl.pallas_call(
        _split_w_in_kernel,
        grid=(k // tr,),
        in_specs=[pl.BlockSpec((tr, n), lambda i: (i, 0))],
        out_specs=[pl.BlockSpec((tr, w), lambda i: (i, 0)) for w in widths],
        out_shape=[jax.ShapeDtypeStruct((k, w), BF16) for w in widths],
        compiler_params=_cparams(1),
        name="split_w_in",
    )(w_in)


def _prep_even(w_in, w_gate_up, b_gate, w_out):
    hk, hv = H_A * DK_A, H_A * DV_A
    w_qkg, w_v, w_og, w_qb, w_kv = _split_w_in(w_in)
    wup = jnp.pad(w_gate_up, ((0, GATE_PAD - GATE_RANK), (0, 0)))
    return dict(
        w_qkg=w_qkg, w_v=w_v, w_og=w_og, w_qb=w_qb, w_kv=w_kv,
        wup=wup.astype(BF16), bg=b_gate.reshape(1, hk),
        w_out_a=w_out[:hv].astype(BF16), w_out_b=w_out[hv:].astype(BF16),
    )


def _prep_ffn(cw, cb, w3_all, layer):
    pad = D_FF_PAD - D_FF
    return dict(layer=layer,
                cw=jnp.pad(cw, ((0, 0), (0, pad))),
                cb=jnp.pad(cb, (0, pad)).reshape(1, D_FF_PAD),
                w3=_cast_pad_rows(w3_all, layer, D_FF_PAD))


def _toeplitz_kernel(ext_ref, o_ref):
    heads, rows, width = o_ref.shape
    for h in range(heads):
        o_ref[h] = pltpu.roll(jnp.broadcast_to(ext_ref[h], (rows, width)), 0, 1, stride=1, stride_axis=0)


def _rel_toeplitz(table, n_rows, n_cols, offset):
    heads = table.shape[0]
    pad = -(-n_rows // LANE) * LANE
    width = -(-(pad + n_cols) // LANE) * LANE
    n_hi = offset + pad - REL_CLIP
    assert n_hi >= 0
    ext = jnp.concatenate([
        jnp.broadcast_to(table[:, -1:], (heads, n_hi)), table[:, ::-1],
        jnp.broadcast_to(table[:, :1], (heads, width))], axis=1)[:, :width]
    toe = pl.pallas_call(
        _toeplitz_kernel,
        grid=(1,),
        in_specs=[pl.BlockSpec((heads, 1, width), lambda i: (0, 0, 0))],
        out_specs=pl.BlockSpec((heads, n_rows, width), lambda i: (0, 0, 0)),
        out_shape=jax.ShapeDtypeStruct((heads, n_rows, width), F32),
        compiler_params=_cparams(1),
        name="rel_toeplitz",
    )(ext.astype(F32).reshape(heads, 1, width))
    return toe, pad


def _prompt_bias(table):
    n_special = BAND_PAST // ATT_QB
    toe, pad = _rel_toeplitz(table, ATT_QB, ATT_KW + BAND_PAST, BAND_PAST)
    r = jnp.arange(ATT_QB)[:, None]
    j = jnp.arange(ATT_KW)[None, :]
    out = []
    for qb in range(n_special + 1):
        q_chunk = (qb * ATT_QB + r) // CHUNK
        valid = (j >= q_chunk * CHUNK - BAND_PAST) & (j < (q_chunk + 1) * CHUNK)
        shift = pad + (n_special - qb) * ATT_QB
        out.append(jnp.where(valid[None], toe[:, :, shift:shift + ATT_KW] * LOG2E, NEG_INF))
    return jnp.stack(out)


def _step_bias(table, T, lc):
    toe, pad = _rel_toeplitz(table, T, lc + T, lc)
    return toe[:, :, pad:pad + lc], toe[:, :, pad + lc:pad + lc + T]


def _conv_ffn(x, xb, F, ln_g, ln_b, hist, *, B, T, tm_up, tm_down):
    pad = D_FF_PAD - D_FF
    keep = CONV_W - 1
    hist8 = jnp.pad(hist, ((0, 0), (SUBLANE - keep, 0), (0, pad)))
    h, st8 = _ffn_up(xb, F["w1_all"], F["w2_all"], F["layer"], hist8, F["cw"], F["cb"], T=T, tm=tm_up)
    new_state = st8.reshape(B, -1, SUBLANE, D_FF_PAD)[:, -1, SUBLANE - keep:, :D_FF]
    y, yb = _ffn_down(h, F["w3"], x, ln_g, ln_b, tm=tm_down)
    return y, yb, new_state


def _trunk(x3, E, Fs, O, norms, state, bias):
    B, T, _ = x3.shape
    m = B * T
    tm = min(512, m)
    tm_lin = min(1024, m)
    tm_up = min(2048, m)
    tm_small = min(256, m)
    x = x3.reshape(m, D_MODEL)
    hv, hd = H_A * DV_A, H_B * DH_B
    prompt = state is None

    qkg, xb = _linear(x, E["w_qkg"], out_dtype=F32, tm=tm_lin, emit_xb=True)
    v_a = _linear(xb, E["w_v"], out_dtype=BF16, tm=tm_lin)
    og = _linear(xb, E["w_og"], out_dtype=F32, tm=tm_lin)
    q_b = _linear(xb, E["w_qb"], out_dtype=BF16, tm=tm_lin)
    if prompt:
        s0_t = jnp.zeros((B, H_A, DV_A, DK_A), F32)
        conv_hist = [jnp.zeros((B, CONV_W - 1, D_FF), F32)] * DEPTH
    else:
        cache_k, cache_v, state_gla, state_conv = state
        s0_t = jnp.swapaxes(state_gla[0], -1, -2)
        conv_hist = [state_conv[i] for i in range(DEPTH)]
    o_a, s_fin_t = _gla(qkg, v_a, og, E["wup"], E["bg"], s0_t, E["gn"], B=B, T=T)
    new_gla = jnp.swapaxes(s_fin_t, -1, -2)[None]
    if prompt:
        keep = min(BAND_PAST, T)
        kv_b, kv_tail = _linear(xb, E["w_kv"], out_dtype=BF16, tm=tm_lin, tail=(T // tm_lin, keep))
        o_b = _band_attn(q_b, kv_b, bias, B=B, T=T)
        kv_rows = kv_tail.reshape(B, keep, 2, H_B, DH_B)
    else:
        kv_f = _linear(xb, E["w_kv"], out_dtype=F32, tm=tm_lin)
        o_b = _attn_step(q_b, kv_f, cache_k, cache_v, bias[0], bias[1], B=B, T=T)
        kv_rows = kv_f.reshape(B, T, 2, H_B, DH_B)
    new_k = kv_rows[:, :, 0][None]
    new_v = kv_rows[:, :, 1][None]
    x, xb = _proj_ln(o_a, o_b, E["w_out_a"], E["w_out_b"], x, norms["ln1_g"][0], norms["ln1_b"][0], tm=tm)
    x, xb, conv0 = _conv_ffn(x, xb, Fs[0], norms["ln2_g"][0], norms["ln2_b"][0], conv_hist[0],
                             B=B, T=T, tm_up=tm_up, tm_down=tm)

    x, xb, *maybe_v = _gmlp(xb, x, O, norms["ln1_g"][1], norms["ln1_b"][1], T=T,
                            tm=tm_small if not prompt else tm, emit_v=not prompt)
    x, xb, conv1 = _conv_ffn(x, xb, Fs[1], norms["ln2_g"][1], norms["ln2_b"][1], conv_hist[1],
                             B=B, T=T, tm_up=tm_up, tm_down=tm)

    y = x.reshape(B, T, D_MODEL)
    new_conv = jnp.stack([conv0, conv1])
    mlp_v = maybe_v[0].reshape(B, T, DC)[None] if maybe_v else None
    return y, new_k, new_v, new_gla, new_conv, mlp_v


def kernel(x_prompt, x_sample, cache_attn_k, cache_attn_v, state_gla, state_ffn_conv, w_in_even, w_gate_up, b_gate, gla_norm_g, rel_bias, w_out_even, w_in_odd, ln_v_g, ln_v_b, w_spatial, b_spatial, w_out_odd, ffn_w1, ffn_w2, ffn_conv_w, ffn_conv_b, ffn_w3, ln1_g, ln1_b, ln2_g, ln2_b):
    E = _prep_even(w_in_even.reshape(w_in_even.shape[-2:]), w_gate_up[0], b_gate[0], w_out_even[0])
    E["gn"] = gla_norm_g[0]
    Fs = [dict(_prep_ffn(ffn_conv_w[i], ffn_conv_b[i], ffn_w3, i), w1_all=ffn_w1, w2_all=ffn_w2)
          for i in range(DEPTH)]
    O = dict(w_in=w_in_odd[0].astype(BF16), ln_v_g=ln_v_g[0], ln_v_b=ln_v_b[0],
             ws=w_spatial[0], bs=b_spatial[0], w_out=w_out_odd[0].astype(BF16))
    norms = dict(ln1_g=ln1_g, ln1_b=ln1_b, ln2_g=ln2_g, ln2_b=ln2_b)

    bias_p = _prompt_bias(rel_bias[0])
    y_p, k_p, v_p, gla_p, conv_p, _ = _trunk(x_prompt, E, Fs, O, norms, None, bias_p)

    bias_s = _step_bias(rel_bias[0], x_sample.shape[1], cache_attn_k.shape[2])
    y_s, k_s, v_s, gla_s, conv_s, mlp_v_s = _trunk(
        x_sample, E, Fs, O, norms, (cache_attn_k, cache_attn_v, state_gla, state_ffn_conv), bias_s)
    return (y_p, y_s, k_p, v_p, gla_p, conv_p, k_s, v_s, gla_s, conv_s, mlp_v_s)
```

```python
import functools

import jax
import jax.numpy as jnp
from jax import lax
from jax.experimental import pallas as pl
from jax.experimental.pallas import tpu as pltpu

F32 = jnp.float32
BF16 = jnp.bfloat16

D_MODEL = 2048
DEPTH = 2
CHUNK = 64
H_A = 4
DK_A = D_MODEL // 16
DV_A = D_MODEL // 8
GATE_RANK = 16
GATE_TEMP = 16.0
H_B = 8
DH_B = D_MODEL // 16
N_PREV_CHUNKS = 8
BAND_PAST = N_PREV_CHUNKS * CHUNK
REL_CLIP = 128
CHUNK_C = 128
DC = D_MODEL
G_C = 8
D_FF = ((8 * D_MODEL // 3 + 127) // 128) * 128
CONV_W = 3
ALPHA = (2 * DEPTH) ** 0.25
LN_EPS = 1e-5
NEG_INF = -1e30
PAST_LEN = 2048
LOG2E = 1.4426950408889634

LANE = 128
SUBLANE = 8
MXU_N = 256
FF_TILE = 512
D_FF_PAD = ((D_FF + FF_TILE - 1) // FF_TILE) * FF_TILE
N_FF_TILES = D_FF_PAD // FF_TILE
GATE_PAD = LANE
QKG_W = 2 * H_A * DK_A + GATE_PAD
ATT_QB = 4 * CHUNK
ATT_KW = BAND_PAST + ATT_QB
ROW_SUB = 128
VMEM_LIMIT = 56 * 1024 * 1024
VMEM_LIMIT_FFN_DOWN = 58 * 1024 * 1024


def _cparams(n_axes, vmem=VMEM_LIMIT, **kw):
    return pltpu.CompilerParams(dimension_semantics=("arbitrary",) * n_axes,
                                vmem_limit_bytes=vmem, **kw)


def _resident(shape):
    nd = len(shape)
    return pl.BlockSpec(shape, lambda *_: (0,) * nd, pipeline_mode=pl.Buffered(1))


def _layer_norm(xf, g, b):
    mu = jnp.mean(xf, -1, keepdims=True)
    xc = xf - mu
    var = jnp.mean(xc * xc, -1, keepdims=True)
    y = xc * lax.rsqrt(var + LN_EPS)
    return y * g + b


def _log_sigmoid(x):
    return jnp.minimum(x, 0.0) - jnp.log1p(jnp.exp(-jnp.abs(x)))


def _linear_kernel(x_ref, w_ref, o_ref, *extra_refs, emit_xb, tail):
    xb = x_ref[...].astype(BF16)
    y = jnp.dot(xb, w_ref[...], preferred_element_type=F32)
    o_ref[...] = y.astype(o_ref.dtype)
    if emit_xb:
        extra_refs[0][...] = xb
    if tail:
        extra_refs[-1][...] = y[y.shape[0] - tail:, :]


def _linear(x, w, *, out_dtype, tm, emit_xb=False, tail=None):
    m, k = x.shape
    n = w.shape[1]
    out_shape = [jax.ShapeDtypeStruct((m, n), out_dtype)]
    out_specs = [pl.BlockSpec((tm, n), lambda i: (i, 0))]
    if emit_xb:
        out_shape.append(jax.ShapeDtypeStruct((m, k), BF16))
        out_specs.append(pl.BlockSpec((tm, k), lambda i: (i, 0)))
    keep = 0
    if tail is not None:
        tps, keep = tail
        assert keep <= tm
        out_shape.append(jax.ShapeDtypeStruct((m // (tm * tps) * keep, n), F32))
        out_specs.append(pl.BlockSpec((keep, n), lambda i: (i // tps, 0)))
    res = pl.pallas_call(
        functools.partial(_linear_kernel, emit_xb=emit_xb, tail=keep),
        grid=(m // tm,),
        in_specs=[pl.BlockSpec((tm, k), lambda i: (i, 0)), _resident(w.shape)],
        out_specs=out_specs,
        out_shape=out_shape,
        compiler_params=_cparams(1),
        name="linear",
    )(x, w)
    return res if len(res) > 1 else res[0]


def _proj_ln_kernel(xa_ref, xb_ref, wa_ref, wb_ref, res_ref, g_ref, b_ref, y_ref, yb_ref, *, tm):
    for r in range(tm // ROW_SUB):
        rows = slice(r * ROW_SUB, (r + 1) * ROW_SUB)
        h = (jnp.dot(xa_ref[rows, :], wa_ref[...], preferred_element_type=F32)
             + jnp.dot(xb_ref[rows, :], wb_ref[...], preferred_element_type=F32))
        y = _layer_norm(ALPHA * res_ref[rows, :] + h, g_ref[...], b_ref[...])
        y_ref[rows, :] = y
        yb_ref[rows, :] = y.astype(BF16)


def _proj_ln(xa, xb, wa, wb, res, g, b, *, tm):
    m, n = res.shape
    row = lambda width: pl.BlockSpec((tm, width), lambda i: (i, 0))
    vec = pl.BlockSpec((1, n), lambda i: (0, 0))
    return pl.pallas_call(
        functools.partial(_proj_ln_kernel, tm=tm),
        grid=(m // tm,),
        in_specs=[row(xa.shape[1]), row(xb.shape[1]), _resident(wa.shape), _resident(wb.shape),
                  row(n), vec, vec],
        out_specs=[row(n), row(n)],
        out_shape=[jax.ShapeDtypeStruct((m, n), F32), jax.ShapeDtypeStruct((m, n), BF16)],
        compiler_params=_cparams(1),
        name="proj_ln",
    )(xa, xb, wa, wb, res, g.reshape(1, n), b.reshape(1, n))


def _gla_kernel(qkg_ref, v_ref, og_ref, wup_ref, bg_ref, s0_ref, gn_ref,
                o_ref, sfin_ref, st_ref, *, L, n_steps, bb, cps):
    c = pl.program_id(1)

    @pl.when(c == 0)
    def _():
        st_ref[...] = s0_ref[...]

    hk = H_A * DK_A
    row = lax.broadcasted_iota(jnp.int32, (L, L), 0)
    col = lax.broadcasted_iota(jnp.int32, (L, L), 1)
    causal = row >= col
    tri = jnp.where(causal, 1.0, 0.0).astype(BF16)
    nt = (((1,), (1,)), ((), ()))
    tn_dims = (((0,), (0,)), ((), ()))

    chains = [(ci, b, h) for ci in range(cps) for b in range(bb) for h in range(H_A)]
    rows = lambda ci: slice(ci * L, (ci + 1) * L)
    bcum = {}
    for ci in range(cps):
        for b in range(bb):
            glow = qkg_ref[b, rows(ci), 2 * hk:2 * hk + GATE_PAD].astype(BF16)
            gate = jnp.dot(glow, wup_ref[...], preferred_element_type=F32) + bg_ref[...]
            logg = _log_sigmoid(gate) * (1.0 / GATE_TEMP)
            hi = logg.astype(BF16)
            r1 = logg - hi.astype(F32)
            mid = r1.astype(BF16)
            lo = (r1 - mid.astype(F32)).astype(BF16)
            bcum[ci, b] = (jnp.dot(tri, hi, preferred_element_type=F32)
                           + jnp.dot(tri, mid, preferred_element_type=F32)
                           + jnp.dot(tri, lo, preferred_element_type=F32))
    q_dec, k_dec, k_upd, decay = {}, {}, {}, {}
    for (ci, b, h) in chains:
        bh = bcum[ci, b][:, h * DK_A:(h + 1) * DK_A]
        qh = qkg_ref[b, rows(ci), h * DK_A:(h + 1) * DK_A]
        kh = qkg_ref[b, rows(ci), hk + h * DK_A:hk + (h + 1) * DK_A]
        b_last = bh[L - 1:L, :]
        q_dec[ci, b, h] = ((qh * (DK_A ** -0.5)) * jnp.exp(bh)).astype(BF16)
        k_dec[ci, b, h] = (kh * jnp.exp(-bh)).astype(BF16)
        k_upd[ci, b, h] = (kh * jnp.exp(b_last - bh)).astype(BF16)
        decay[ci, b, h] = jnp.exp(b_last)
    att = {}
    for key in chains:
        a = lax.dot_general(q_dec[key], k_dec[key], nt, preferred_element_type=F32)
        att[key] = jnp.where(causal, a, 0.0).astype(BF16)
    outs = {}
    for (ci, b, h) in chains:
        key = (ci, b, h)
        vh = v_ref[b, rows(ci), h * DV_A:(h + 1) * DV_A]
        st = st_ref[b, h]
        outs[key] = (lax.dot_general(q_dec[key], st.astype(BF16), nt, preferred_element_type=F32)
                     + jnp.dot(att[key], vh, preferred_element_type=F32))
        st_ref[b, h] = st * decay[key] + lax.dot_general(
            vh, k_upd[key], tn_dims, preferred_element_type=F32)
    for (ci, b, h) in chains:
        vs = slice(h * DV_A, (h + 1) * DV_A)
        o = outs[ci, b, h]
        on = o * lax.rsqrt(jnp.mean(o * o, -1, keepdims=True) + LN_EPS) * gn_ref[h:h + 1, :]
        o_ref[b, rows(ci), vs] = (on * jax.nn.silu(og_ref[b, rows(ci), vs])).astype(BF16)

    @pl.when(c == n_steps - 1)
    def _():
        sfin_ref[...] = st_ref[...]


def _gla(qkg, v_a, og, wup, bg, s0_t, gn, *, B, T):
    L = min(T, CHUNK)
    n = T // L
    cps = 2 if n % 2 == 0 else 1
    bb = min(B, 4)
    hv = H_A * DV_A
    seq = lambda width: pl.BlockSpec((bb, cps * L, width), lambda g, c: (g, c, 0))
    const2 = lambda g, c: (0, 0)
    state = pl.BlockSpec((bb, H_A, DV_A, DK_A), lambda g, c: (g, 0, 0, 0))
    o, s_fin = pl.pallas_call(
        functools.partial(_gla_kernel, L=L, n_steps=n // cps, bb=bb, cps=cps),
        grid=(B // bb, n // cps),
        in_specs=[seq(QKG_W), seq(hv), seq(hv),
                  pl.BlockSpec(wup.shape, const2), pl.BlockSpec(bg.shape, const2),
                  state, pl.BlockSpec(gn.shape, const2)],
        out_specs=[seq(hv), state],
        out_shape=[jax.ShapeDtypeStruct((B, T, hv), BF16),
                   jax.ShapeDtypeStruct((B, H_A, DV_A, DK_A), F32)],
        scratch_shapes=[pltpu.VMEM((bb, H_A, DV_A, DK_A), F32)],
        compiler_params=_cparams(2),
        name="gla",
    )(qkg.reshape(B, T, QKG_W), v_a.reshape(B, T, hv), og.reshape(B, T, hv), wup, bg, s0_t, gn)
    return o.reshape(B * T, hv), s_fin


def _band_attn_kernel(q_ref, kv_ref, bias_ref, o_ref):
    qb = pl.program_id(1)
    start = pl.multiple_of(jnp.maximum(qb * ATT_QB - BAND_PAST, 0), ATT_QB)
    hd = H_B * DH_B
    nt = (((1,), (1,)), ((), ()))

    def logits2(h):
        q = q_ref[:, h * DH_B:(h + 1) * DH_B]
        k = kv_ref[pl.ds(start, ATT_KW), h * DH_B:(h + 1) * DH_B]
        return (lax.dot_general(q, k, nt, preferred_element_type=F32) * (DH_B ** -0.5 * LOG2E)
                + bias_ref[0, h])

    def weights(t):
        e = jnp.exp2(t - jnp.max(t, -1, keepdims=True))
        return e.astype(BF16), jnp.sum(e, -1, keepdims=True)

    def output(h, e, den):
        v = kv_ref[pl.ds(start, ATT_KW), hd + h * DH_B:hd + (h + 1) * DH_B]
        o = jnp.dot(e, v, preferred_element_type=F32) / den
        o_ref[:, h * DH_B:(h + 1) * DH_B] = o.astype(BF16)

    t_next = logits2(0)
    for h in range(H_B):
        t_cur = t_next
        if h + 1 < H_B:
            t_next = logits2(h + 1)
        output(h, *weights(t_cur))


def _band_attn(q_b, kv_b, bias_blocks, *, B, T):
    assert T % ATT_QB == 0 and T >= ATT_KW
    n = T // ATT_QB
    hd = H_B * DH_B
    last_bias = bias_blocks.shape[0] - 1
    return pl.pallas_call(
        _band_attn_kernel,
        grid=(B, n),
        in_specs=[
            pl.BlockSpec((ATT_QB, hd), lambda b, c: (b * n + c, 0)),
            pl.BlockSpec((T, 2 * hd), lambda b, c: (b, 0)),
            pl.BlockSpec((1, H_B, ATT_QB, ATT_KW), lambda b, c: (jnp.minimum(c, last_bias), 0, 0, 0)),
        ],
        out_specs=pl.BlockSpec((ATT_QB, hd), lambda b, c: (b * n + c, 0)),
        out_shape=jax.ShapeDtypeStruct((B * T, hd), BF16),
        compiler_params=_cparams(2),
        name="band_attn",
    )(q_b, kv_b, bias_blocks)


def _attn_step_kernel(q_ref, kv_ref, ck_ref, cv_ref, bias_c_ref, bias_n_ref, o_ref):
    hd = H_B * DH_B
    lc = ck_ref.shape[0] // H_B
    nt = (((1,), (1,)), ((), ()))
    scale = DH_B ** -0.5
    def scores(h):
        hs = slice(h * DH_B, (h + 1) * DH_B)
        q = q_ref[:, hs].astype(BF16)
        kc = ck_ref[pl.ds(h, lc, stride=H_B), :].astype(BF16)
        kn = kv_ref[:, hs].astype(BF16)
        sc = lax.dot_general(q, kc, nt, preferred_element_type=F32) * scale + bias_c_ref[h]
        sn = lax.dot_general(q, kn, nt, preferred_element_type=F32) * scale + bias_n_ref[h]
        return sc, sn

    def output(h, sc, sn):
        vc = cv_ref[pl.ds(h, lc, stride=H_B), :].astype(BF16)
        vn = kv_ref[:, hd + h * DH_B:hd + (h + 1) * DH_B].astype(BF16)
        m = jnp.maximum(jnp.max(sc, -1, keepdims=True), jnp.max(sn, -1, keepdims=True))
        ec = jnp.exp(sc - m)
        en = jnp.exp(sn - m)
        den = jnp.sum(ec, -1, keepdims=True) + jnp.sum(en, -1, keepdims=True)
        o = (jnp.dot((ec / den).astype(BF16), vc, preferred_element_type=F32)
             + jnp.dot((en / den).astype(BF16), vn, preferred_element_type=F32))
        o_ref[:, h * DH_B:(h + 1) * DH_B] = o.astype(BF16)

    s_next = scores(0)
    for h in range(H_B):
        s_cur = s_next
        if h + 1 < H_B:
            s_next = scores(h + 1)
        output(h, *s_cur)


def _attn_step(q_b, kv_b, cache_k, cache_v, bias_c, bias_n, *, B, T):
    hd = H_B * DH_B
    lc = cache_k.shape[2]
    ck = cache_k.reshape(1, B, lc * H_B, DH_B)
    cv = cache_v.reshape(1, B, lc * H_B, DH_B)
    cache = pl.BlockSpec((None, None, lc * H_B, DH_B), lambda b: (0, b, 0, 0))
    return pl.pallas_call(
        _attn_step_kernel,
        grid=(B,),
        in_specs=[
            pl.BlockSpec((T, hd), lambda b: (b, 0)),
            pl.BlockSpec((T, 2 * hd), lambda b: (b, 0)),
            cache, cache,
            pl.BlockSpec(bias_c.shape, lambda b: (0, 0, 0)),
            pl.BlockSpec(bias_n.shape, lambda b: (0, 0, 0)),
        ],
        out_specs=pl.BlockSpec((T, hd), lambda b: (b, 0)),
        out_shape=jax.ShapeDtypeStruct((B * T, hd), BF16),
        compiler_params=_cparams(1),
        name="attn_step",
    )(q_b, kv_b, ck, cv, bias_c, bias_n)


def _gmlp_kernel(xb_ref, x_ref, win_ref, lvg_ref, lvb_ref, ws_ref, bs_ref, wout_ref, g_ref, b_ref,
                 y_ref, yb_ref, *maybe_v_ref, tm, L):
    row = lax.broadcasted_iota(jnp.int32, (L, L), 0)
    col = lax.broadcasted_iota(jnp.int32, (L, L), 1)
    causal = row >= col
    gw = DC // G_C
    wms = [jnp.where(causal, ws_ref[g], 0.0).astype(BF16) for g in range(G_C)]
    sub = max(ROW_SUB, L)
    n_sub = tm // sub
    mm_in = lambda r: jnp.dot(xb_ref[r * sub:(r + 1) * sub, :], win_ref[...], preferred_element_type=F32)
    uv_next = mm_in(0)
    for r in range(n_sub):
        rows = slice(r * sub, (r + 1) * sub)
        uv_raw = uv_next
        if r + 1 < n_sub:
            uv_next = mm_in(r + 1)
        uv = jax.nn.gelu(uv_raw)
        u = uv[:, :DC]
        v = _layer_norm(uv[:, DC:], lvg_ref[...], lvb_ref[...])
        if maybe_v_ref:
            maybe_v_ref[0][rows, :] = v
        vb = v.astype(BF16)
        chunks = []
        for c in range(sub // L):
            cr = slice(c * L, (c + 1) * L)
            groups = []
            for g in range(G_C):
                cs = slice(g * gw, (g + 1) * gw)
                sv = jnp.dot(wms[g], vb[cr, cs], preferred_element_type=F32) + bs_ref[:, g:g + 1]
                groups.append((u[cr, cs] * sv).astype(BF16))
            chunks.append(jnp.concatenate(groups, axis=1))
        gated = chunks[0] if len(chunks) == 1 else jnp.concatenate(chunks, axis=0)
        h = jnp.dot(gated, wout_ref[...], preferred_element_type=F32)
        y = _layer_norm(ALPHA * x_ref[rows, :] + h, g_ref[...], b_ref[...])
        y_ref[rows, :] = y
        yb_ref[rows, :] = y.astype(BF16)


def _gmlp(xb, x, O, g, b, *, T, tm, emit_v):
    m = x.shape[0]
    L = min(T, CHUNK_C)
    ws = O["ws"][:, :L, :L]
    bs_t = jnp.transpose(O["bs"][:, :L])
    row = pl.BlockSpec((tm, D_MODEL), lambda i: (i, 0))
    vec = pl.BlockSpec((1, D_MODEL), lambda i: (0, 0))
    return pl.pallas_call(
        functools.partial(_gmlp_kernel, tm=tm, L=L),
        grid=(m // tm,),
        in_specs=[row, row, _resident(O["w_in"].shape), vec, vec,
                  pl.BlockSpec(ws.shape, lambda i: (0, 0, 0)), pl.BlockSpec(bs_t.shape, lambda i: (0, 0)),
                  _resident(O["w_out"].shape), vec, vec],
        out_specs=[row] * (3 if emit_v else 2),
        out_shape=[jax.ShapeDtypeStruct((m, D_MODEL), F32), jax.ShapeDtypeStruct((m, D_MODEL), BF16)]
        + ([jax.ShapeDtypeStruct((m, DC), F32)] if emit_v else []),
        compiler_params=_cparams(1),
        name="gmlp",
    )(xb, x, O["w_in"], O["ln_v_g"].reshape(1, DC), O["ln_v_b"].reshape(1, DC), ws, bs_t,
      O["w_out"], g.reshape(1, D_MODEL), b.reshape(1, D_MODEL))


def _ffn_up_kernel(x_ref, w1_ref, w2_ref, hist_ref, cw_ref, cb_ref, h_ref, st_ref,
                   wb_ref, carry_ref, *, tm, seg, tiles_per_seq):
    j = pl.program_id(0)
    i = pl.program_id(1)
    tn = FF_TILE

    @pl.when(i == 0)
    def _():
        keep = j * tn + lax.broadcasted_iota(jnp.int32, (1, tn), 1) < D_FF
        wb_ref[:, :tn] = jnp.where(keep, w1_ref[...], 0.0).astype(BF16)
        wb_ref[:, tn:] = jnp.where(keep, w2_ref[...], 0.0).astype(BF16)
        carry_ref[...] = jnp.zeros_like(carry_ref)

    n_seg = tm // seg
    rg = min(tm, 256)
    piece = min(seg, rg)
    seq_start = i % tiles_per_seq == 0
    sub_i = lax.broadcasted_iota(jnp.int32, (SUBLANE, MXU_N), 0)
    for c in range(tn // MXU_N):
        cs = slice(c * MXU_N, (c + 1) * MXU_N)
        prev = None
        for r in range(tm // rg):
            x = x_ref[r * rg:(r + 1) * rg, :]
            u = jnp.dot(x, wb_ref[:, c * MXU_N:(c + 1) * MXU_N], preferred_element_type=F32)
            z = jnp.dot(x, wb_ref[:, tn + c * MXU_N:tn + (c + 1) * MXU_N], preferred_element_type=F32)
            for p in range(rg // piece):
                row0 = r * rg + p * piece
                s = row0 // seg
                if row0 % seg == 0:
                    if n_seg == 1:
                        prev = jnp.where(seq_start, hist_ref[0, :, cs], carry_ref[:, cs])
                    else:
                        prev = hist_ref[s, :, cs]
                up = u[p * piece:(p + 1) * piece]
                zp = z[p * piece:(p + 1) * piece]
                u1 = pltpu.roll(up, 1, 0)
                u2 = pltpu.roll(up, 2, 0)
                h1 = jnp.where(sub_i < 1, pltpu.roll(prev, 1, 0), u1[:SUBLANE])
                h2 = jnp.where(sub_i < 2, pltpu.roll(prev, 2, 0), u2[:SUBLANE])
                u1 = jnp.concatenate([h1, u1[SUBLANE:]], axis=0)
                u2 = jnp.concatenate([h2, u2[SUBLANE:]], axis=0)
                cv = u2 * cw_ref[0:1, cs] + u1 * cw_ref[1:2, cs] + up * cw_ref[2:3, cs]
                h_ref[row0:row0 + piece, cs] = (jax.nn.gelu(cv + cb_ref[:, cs]) * zp).astype(BF16)
                prev = up[piece - SUBLANE:]
                if (row0 + piece) % seg == 0:
                    st_ref[s, :, cs] = prev
                    if n_seg == 1:
                        carry_ref[:, cs] = prev


def _ffn_up(xb, w1_all, w2_all, layer, hist8, cw, cb, *, T, tm):
    m = xb.shape[0]
    tn = FF_TILE
    seg = min(T, tm)
    n_seg = tm // seg
    tps = T // seg
    wspec = pl.BlockSpec((None, D_MODEL, tn), lambda j, i: (layer, 0, j))
    return pl.pallas_call(
        functools.partial(_ffn_up_kernel, tm=tm, seg=seg, tiles_per_seq=tps),
        grid=(N_FF_TILES, m // tm),
        in_specs=[
            pl.BlockSpec((tm, D_MODEL), lambda j, i: (i, 0)),
            wspec, wspec,
            pl.BlockSpec((n_seg, SUBLANE, tn), lambda j, i: (i // tps, 0, j)),
            pl.BlockSpec((CONV_W, tn), lambda j, i: (0, j)),
            pl.BlockSpec((1, tn), lambda j, i: (0, j)),
        ],
        out_specs=[
            pl.BlockSpec((tm, tn), lambda j, i: (i, j)),
            pl.BlockSpec((n_seg, SUBLANE, tn), lambda j, i: (i, 0, j)),
        ],
        out_shape=[
            jax.ShapeDtypeStruct((m, D_FF_PAD), BF16),
            jax.ShapeDtypeStruct((m // seg, SUBLANE, D_FF_PAD), F32),
        ],
        scratch_shapes=[pltpu.VMEM((D_MODEL, 2 * tn), BF16), pltpu.VMEM((SUBLANE, tn), F32)],
        compiler_params=_cparams(2),
        name="ffn_up",
    )(xb, w1_all, w2_all, hist8, cw, cb)


def _ffn_down_kernel(h_ref, w_ref, res_ref, g_ref, b_ref, y_ref, yb_ref, *, tm):
    for r in range(tm // ROW_SUB):
        rows = slice(r * ROW_SUB, (r + 1) * ROW_SUB)
        acc = jnp.dot(h_ref[rows, :], w_ref[...], preferred_element_type=F32)
        y = _layer_norm(ALPHA * res_ref[rows, :] + acc, g_ref[...], b_ref[...])
        y_ref[rows, :] = y
        yb_ref[rows, :] = y.astype(BF16)


def _ffn_down(h, w3, res, g, b, *, tm):
    m, n = res.shape
    row = lambda width: pl.BlockSpec((tm, width), lambda i: (i, 0))
    vec = pl.BlockSpec((1, n), lambda i: (0, 0))
    return pl.pallas_call(
        functools.partial(_ffn_down_kernel, tm=tm),
        grid=(m // tm,),
        in_specs=[row(D_FF_PAD), _resident(w3.shape), row(n), vec, vec],
        out_specs=[row(n), row(n)],
        out_shape=[jax.ShapeDtypeStruct((m, n), F32), jax.ShapeDtypeStruct((m, n), BF16)],
        compiler_params=_cparams(1, vmem=VMEM_LIMIT_FFN_DOWN),
        name="ffn_down",
    )(h, w3, res, g.reshape(1, n), b.reshape(1, n))


def _cast_pad_rows_kernel(w_ref, o_ref, *, rows_valid, tr):
    k = pl.program_id(0)
    keep = k * tr + lax.broadcasted_iota(jnp.int32, (tr, 1), 0) < rows_valid
    o_ref[...] = jnp.where(keep, w_ref[...], 0.0).astype(BF16)


def _cast_pad_rows(w_all, layer, rows_pad, *, tr=FF_TILE):
    _, rows, n = w_all.shape
    return pl.pallas_call(
        functools.partial(_cast_pad_rows_kernel, rows_valid=rows, tr=tr),
        grid=(rows_pad // tr,),
        in_specs=[pl.BlockSpec((None, tr, n), lambda k: (layer, k, 0))],
        out_specs=pl.BlockSpec((tr, n), lambda k: (k, 0)),
        out_shape=jax.ShapeDtypeStruct((rows_pad, n), BF16),
        compiler_params=_cparams(1),
        name="cast_pad_rows",
    )(w_all)


def _split_w_in_kernel(w_ref, qkg_ref, v_ref, og_ref, qb_ref, kv_ref):
    hk, hv, hd = H_A * DK_A, H_A * DV_A, H_B * DH_B
    qkg_ref[:, :2 * hk] = w_ref[:, :2 * hk].astype(BF16)
    o = 2 * hk
    v_ref[...] = w_ref[:, o:o + hv].astype(BF16)
    o += hv
    g = w_ref[:, o:o + GATE_PAD]
    lane = lax.broadcasted_iota(jnp.int32, g.shape, 1)
    qkg_ref[:, 2 * hk:] = jnp.where(lane < GATE_RANK, g, 0.0).astype(BF16)
    o += GATE_RANK
    og_ref[...] = w_ref[:, o:o + hv].astype(BF16)
    o += hv
    qb_ref[...] = w_ref[:, o:o + hd].astype(BF16)
    o += hd
    kv_ref[...] = w_ref[:, o:o + 2 * hd].astype(BF16)


def _split_w_in(w_in, *, tr=256):
    k, n = w_in.shape
    hv, hd = H_A * DV_A, H_B * DH_B
    widths = [QKG_W, hv, hv, hd, 2 * hd]
    return pl.pallas_call(
        _split_w_in_kernel,
        grid=(k // tr,),
        in_specs=[pl.BlockSpec((tr, n), lambda i: (i, 0))],
        out_specs=[pl.BlockSpec((tr, w), lambda i: (i, 0)) for w in widths],
        out_shape=[jax.ShapeDtypeStruct((k, w), BF16) for w in widths],
        compiler_params=_cparams(1),
        name="split_w_in",
    )(w_in)


def _prep_even(w_in, w_gate_up, b_gate, w_out):
    hk, hv = H_A * DK_A, H_A * DV_A
    w_qkg, w_v, w_og, w_qb, w_kv = _split_w_in(w_in)
    wup = jnp.pad(w_gate_up, ((0, GATE_PAD - GATE_RANK), (0, 0)))
    return dict(
        w_qkg=w_qkg, w_v=w_v, w_og=w_og, w_qb=w_qb, w_kv=w_kv,
        wup=wup.astype(BF16), bg=b_gate.reshape(1, hk),
        w_out_a=w_out[:hv].astype(BF16), w_out_b=w_out[hv:].astype(BF16),
    )


def _prep_ffn(cw, cb, w3_all, layer):
    pad = D_FF_PAD - D_FF
    return dict(layer=layer,
                cw=jnp.pad(cw, ((0, 0), (0, pad))),
                cb=jnp.pad(cb, (0, pad)).reshape(1, D_FF_PAD),
                w3=_cast_pad_rows(w3_all, layer, D_FF_PAD))


def _toeplitz_kernel(ext_ref, o_ref):
    heads, rows, width = o_ref.shape
    for h in range(heads):
        o_ref[h] = pltpu.roll(jnp.broadcast_to(ext_ref[h], (rows, width)), 0, 1, stride=1, stride_axis=0)


def _rel_toeplitz(table, n_rows, n_cols, offset):
    heads = table.shape[0]
    pad = -(-n_rows // LANE) * LANE
    width = -(-(pad + n_cols) // LANE) * LANE
    n_hi = offset + pad - REL_CLIP
    assert n_hi >= 0
    ext = jnp.concatenate([
        jnp.broadcast_to(table[:, -1:], (heads, n_hi)), table[:, ::-1],
        jnp.broadcast_to(table[:, :1], (heads, width))], axis=1)[:, :width]
    toe = pl.pallas_call(
        _toeplitz_kernel,
        grid=(1,),
        in_specs=[pl.BlockSpec((heads, 1, width), lambda i: (0, 0, 0))],
        out_specs=pl.BlockSpec((heads, n_rows, width), lambda i: (0, 0, 0)),
        out_shape=jax.ShapeDtypeStruct((heads, n_rows, width), F32),
        compiler_params=_cparams(1),
        name="rel_toeplitz",
    )(ext.astype(F32).reshape(heads, 1, width))
    return toe, pad


def _prompt_bias(table):
    n_special = BAND_PAST // ATT_QB
    toe, pad = _rel_toeplitz(table, ATT_QB, ATT_KW + BAND_PAST, BAND_PAST)
    r = jnp.arange(ATT_QB)[:, None]
    j = jnp.arange(ATT_KW)[None, :]
    out = []
    for qb in range(n_special + 1):
        q_chunk = (qb * ATT_QB + r) // CHUNK
        valid = (j >= q_chunk * CHUNK - BAND_PAST) & (j < (q_chunk + 1) * CHUNK)
        shift = pad + (n_special - qb) * ATT_QB
        out.append(jnp.where(valid[None], toe[:, :, shift:shift + ATT_KW] * LOG2E, NEG_INF))
    return jnp.stack(out)


def _step_bias(table, T, lc):
    toe, pad = _rel_toeplitz(table, T, lc + T, lc)
    return toe[:, :, pad:pad + lc], toe[:, :, pad + lc:pad + lc + T]


def _conv_ffn(x, xb, F, ln_g, ln_b, hist, *, B, T, tm_up, tm_down):
    pad = D_FF_PAD - D_FF
    keep = CONV_W - 1
    hist8 = jnp.pad(hist, ((0, 0), (SUBLANE - keep, 0), (0, pad)))
    h, st8 = _ffn_up(xb, F["w1_all"], F["w2_all"], F["layer"], hist8, F["cw"], F["cb"], T=T, tm=tm_up)
    new_state = st8.reshape(B, -1, SUBLANE, D_FF_PAD)[:, -1, SUBLANE - keep:, :D_FF]
    y, yb = _ffn_down(h, F["w3"], x, ln_g, ln_b, tm=tm_down)
    return y, yb, new_state


def _trunk(x3, E, Fs, O, norms, state, bias):
    B, T, _ = x3.shape
    m = B * T
    tm = min(512, m)
    tm_lin = min(1024, m)
    tm_up = min(2048, m)
    tm_small = min(256, m)
    x = x3.reshape(m, D_MODEL)
    hv, hd = H_A * DV_A, H_B * DH_B
    prompt = state is None

    qkg, xb = _linear(x, E["w_qkg"], out_dtype=F32, tm=tm_lin, emit_xb=True)
    v_a = _linear(xb, E["w_v"], out_dtype=BF16, tm=tm_lin)
    og = _linear(xb, E["w_og"], out_dtype=F32, tm=tm_lin)
    q_b = _linear(xb, E["w_qb"], out_dtype=BF16, tm=tm_lin)
    if prompt:
        s0_t = jnp.zeros((B, H_A, DV_A, DK_A), F32)
        conv_hist = [jnp.zeros((B, CONV_W - 1, D_FF), F32)] * DEPTH
    else:
        cache_k, cache_v, state_gla, state_conv = state
        s0_t = jnp.swapaxes(state_gla[0], -1, -2)
        conv_hist = [state_conv[i] for i in range(DEPTH)]
    o_a, s_fin_t = _gla(qkg, v_a, og, E["wup"], E["bg"], s0_t, E["gn"], B=B, T=T)
    new_gla = jnp.swapaxes(s_fin_t, -1, -2)[None]
    if prompt:
        keep = min(BAND_PAST, T)
        kv_b, kv_tail = _linear(xb, E["w_kv"], out_dtype=BF16, tm=tm_lin, tail=(T // tm_lin, keep))
        o_b = _band_attn(q_b, kv_b, bias, B=B, T=T)
        kv_rows = kv_tail.reshape(B, keep, 2, H_B, DH_B)
    else:
        kv_f = _linear(xb, E["w_kv"], out_dtype=F32, tm=tm_lin)
        o_b = _attn_step(q_b, kv_f, cache_k, cache_v, bias[0], bias[1], B=B, T=T)
        kv_rows = kv_f.reshape(B, T, 2, H_B, DH_B)
    new_k = kv_rows[:, :, 0][None]
    new_v = kv_rows[:, :, 1][None]
    x, xb = _proj_ln(o_a, o_b, E["w_out_a"], E["w_out_b"], x, norms["ln1_g"][0], norms["ln1_b"][0], tm=tm)
    x, xb, conv0 = _conv_ffn(x, xb, Fs[0], norms["ln2_g"][0], norms["ln2_b"][0], conv_hist[0],
                             B=B, T=T, tm_up=tm_up, tm_down=tm)

    x, xb, *maybe_v = _gmlp(xb, x, O, norms["ln1_g"][1], norms["ln1_b"][1], T=T,
                            tm=tm_small, emit_v=not prompt)
    x, xb, conv1 = _conv_ffn(x, xb, Fs[1], norms["ln2_g"][1], norms["ln2_b"][1], conv_hist[1],
                             B=B, T=T, tm_up=tm_up, tm_down=tm)

    y = x.reshape(B, T, D_MODEL)
    new_conv = jnp.stack([conv0, conv1])
    mlp_v = maybe_v[0].reshape(B, T, DC)[None] if maybe_v else None
    return y, new_k, new_v, new_gla, new_conv, mlp_v


def kernel(x_prompt, x_sample, cache_attn_k, cache_attn_v, state_gla, state_ffn_conv, w_in_even, w_gate_up, b_gate, gla_norm_g, rel_bias, w_out_even, w_in_odd, ln_v_g, ln_v_b, w_spatial, b_spatial, w_out_odd, ffn_w1, ffn_w2, ffn_conv_w, ffn_conv_b, ffn_w3, ln1_g, ln1_b, ln2_g, ln2_b):
    E = _prep_even(w_in_even.reshape(w_in_even.shape[-2:]), w_gate_up[0], b_gate[0], w_out_even[0])
    E["gn"] = gla_norm_g[0]
    Fs = [dict(_prep_ffn(ffn_conv_w[i], ffn_conv_b[i], ffn_w3, i), w1_all=ffn_w1, w2_all=ffn_w2)
          for i in range(DEPTH)]
    O = dict(w_in=w_in_odd[0].astype(BF16), ln_v_g=ln_v_g[0], ln_v_b=ln_v_b[0],
             ws=w_spatial[0], bs=b_spatial[0], w_out=w_out_odd[0].astype(BF16))
    norms = dict(ln1_g=ln1_g, ln1_b=ln1_b, ln2_g=ln2_g, ln2_b=ln2_b)

    bias_p = _prompt_bias(rel_bias[0])
    y_p, k_p, v_p, gla_p, conv_p, _ = _trunk(x_prompt, E, Fs, O, norms, None, bias_p)

    bias_s = _step_bias(rel_bias[0], x_sample.shape[1], cache_attn_k.shape[2])
    y_s, k_s, v_s, gla_s, conv_s, mlp_v_s = _trunk(
        x_sample, E, Fs, O, norms, (cache_attn_k, cache_attn_v, state_gla, state_ffn_conv), bias_s)
    return (y_p, y_s, k_p, v_p, gla_p, conv_p, k_s, v_s, gla_s, conv_s, mlp_v_s)
```

```python
import functools

import jax
import jax.numpy as jnp
from jax import lax
from jax.experimental import pallas as pl
from jax.experimental.pallas import tpu as pltpu

F32 = jnp.float32
BF16 = jnp.bfloat16

D_MODEL = 2048
DEPTH = 2
CHUNK = 64
H_A = 4
DK_A = D_MODEL // 16
DV_A = D_MODEL // 8
GATE_RANK = 16
GATE_TEMP = 16.0
H_B = 8
DH_B = D_MODEL // 16
N_PREV_CHUNKS = 8
BAND_PAST = N_PREV_CHUNKS * CHUNK
REL_CLIP = 128
CHUNK_C = 128
DC = D_MODEL
G_C = 8
D_FF = ((8 * D_MODEL // 3 + 127) // 128) * 128
CONV_W = 3
ALPHA = (2 * DEPTH) ** 0.25
LN_EPS = 1e-5
NEG_INF = -1e30
PAST_LEN = 2048
LOG2E = 1.4426950408889634

LANE = 128
SUBLANE = 8
MXU_N = 256
FF_TILE = 512
D_FF_PAD = ((D_FF + FF_TILE - 1) // FF_TILE) * FF_TILE
N_FF_TILES = D_FF_PAD // FF_TILE
GATE_PAD = LANE
QKG_W = 2 * H_A * DK_A + GATE_PAD
ATT_QB = 4 * CHUNK
ATT_KW = BAND_PAST + ATT_QB
ROW_SUB = 128
VMEM_LIMIT = 56 * 1024 * 1024
VMEM_LIMIT_FFN_DOWN = 58 * 1024 * 1024


def _cparams(n_axes, vmem=VMEM_LIMIT, **kw):
    return pltpu.CompilerParams(dimension_semantics=("arbitrary",) * n_axes,
                                vmem_limit_bytes=vmem, **kw)


def _resident(shape):
    nd = len(shape)
    return pl.BlockSpec(shape, lambda *_: (0,) * nd, pipeline_mode=pl.Buffered(1))


def _layer_norm(xf, g, b):
    mu = jnp.mean(xf, -1, keepdims=True)
    xc = xf - mu
    var = jnp.mean(xc * xc, -1, keepdims=True)
    y = xc * lax.rsqrt(var + LN_EPS)
    return y * g + b


def _log_sigmoid(x):
    return jnp.minimum(x, 0.0) - jnp.log1p(jnp.exp(-jnp.abs(x)))


def _linear_kernel(x_ref, w_ref, o_ref, *extra_refs, emit_xb, tail):
    xb = x_ref[...].astype(BF16)
    y = jnp.dot(xb, w_ref[...], preferred_element_type=F32)
    o_ref[...] = y.astype(o_ref.dtype)
    if emit_xb:
        extra_refs[0][...] = xb
    if tail:
        extra_refs[-1][...] = y[y.shape[0] - tail:, :]


def _linear(x, w, *, out_dtype, tm, emit_xb=False, tail=None):
    m, k = x.shape
    n = w.shape[1]
    out_shape = [jax.ShapeDtypeStruct((m, n), out_dtype)]
    out_specs = [pl.BlockSpec((tm, n), lambda i: (i, 0))]
    if emit_xb:
        out_shape.append(jax.ShapeDtypeStruct((m, k), BF16))
        out_specs.append(pl.BlockSpec((tm, k), lambda i: (i, 0)))
    keep = 0
    if tail is not None:
        tps, keep = tail
        assert keep <= tm
        out_shape.append(jax.ShapeDtypeStruct((m // (tm * tps) * keep, n), F32))
        out_specs.append(pl.BlockSpec((keep, n), lambda i: (i // tps, 0)))
    res = pl.pallas_call(
        functools.partial(_linear_kernel, emit_xb=emit_xb, tail=keep),
        grid=(m // tm,),
        in_specs=[pl.BlockSpec((tm, k), lambda i: (i, 0)), _resident(w.shape)],
        out_specs=out_specs,
        out_shape=out_shape,
        compiler_params=_cparams(1),
        name="linear",
    )(x, w)
    return res if len(res) > 1 else res[0]


def _proj_ln_kernel(xa_ref, xb_ref, wa_ref, wb_ref, res_ref, g_ref, b_ref, y_ref, yb_ref, *, tm):
    for r in range(tm // ROW_SUB):
        rows = slice(r * ROW_SUB, (r + 1) * ROW_SUB)
        h = (jnp.dot(xa_ref[rows, :], wa_ref[...], preferred_element_type=F32)
             + jnp.dot(xb_ref[rows, :], wb_ref[...], preferred_element_type=F32))
        y = _layer_norm(ALPHA * res_ref[rows, :] + h, g_ref[...], b_ref[...])
        y_ref[rows, :] = y
        yb_ref[rows, :] = y.astype(BF16)


def _proj_ln(xa, xb, wa, wb, res, g, b, *, tm):
    m, n = res.shape
    row = lambda width: pl.BlockSpec((tm, width), lambda i: (i, 0))
    vec = pl.BlockSpec((1, n), lambda i: (0, 0))
    return pl.pallas_call(
        functools.partial(_proj_ln_kernel, tm=tm),
        grid=(m // tm,),
        in_specs=[row(xa.shape[1]), row(xb.shape[1]), _resident(wa.shape), _resident(wb.shape),
                  row(n), vec, vec],
        out_specs=[row(n), row(n)],
        out_shape=[jax.ShapeDtypeStruct((m, n), F32), jax.ShapeDtypeStruct((m, n), BF16)],
        compiler_params=_cparams(1),
        name="proj_ln",
    )(xa, xb, wa, wb, res, g.reshape(1, n), b.reshape(1, n))


def _gla_kernel(qkg_ref, v_ref, og_ref, wup_ref, bg_ref, s0_ref, gn_ref,
                o_ref, sfin_ref, st_ref, *, L, n_steps, bb, cps):
    c = pl.program_id(1)

    @pl.when(c == 0)
    def _():
        st_ref[...] = s0_ref[...]

    hk = H_A * DK_A
    row = lax.broadcasted_iota(jnp.int32, (L, L), 0)
    col = lax.broadcasted_iota(jnp.int32, (L, L), 1)
    causal = row >= col
    tri = jnp.where(causal, 1.0, 0.0).astype(BF16)
    nt = (((1,), (1,)), ((), ()))
    tn_dims = (((0,), (0,)), ((), ()))

    chains = [(ci, b, h) for ci in range(cps) for b in range(bb) for h in range(H_A)]
    rows = lambda ci: slice(ci * L, (ci + 1) * L)
    bcum = {}
    for ci in range(cps):
        for b in range(bb):
            glow = qkg_ref[b, rows(ci), 2 * hk:2 * hk + GATE_PAD].astype(BF16)
            gate = jnp.dot(glow, wup_ref[...], preferred_element_type=F32) + bg_ref[...]
            logg = _log_sigmoid(gate) * (1.0 / GATE_TEMP)
            hi = logg.astype(BF16)
            r1 = logg - hi.astype(F32)
            mid = r1.astype(BF16)
            lo = (r1 - mid.astype(F32)).astype(BF16)
            bcum[ci, b] = (jnp.dot(tri, hi, preferred_element_type=F32)
                           + jnp.dot(tri, mid, preferred_element_type=F32)
                           + jnp.dot(tri, lo, preferred_element_type=F32))
    q_dec, k_dec, k_upd, decay = {}, {}, {}, {}
    for (ci, b, h) in chains:
        bh = bcum[ci, b][:, h * DK_A:(h + 1) * DK_A]
        qh = qkg_ref[b, rows(ci), h * DK_A:(h + 1) * DK_A]
        kh = qkg_ref[b, rows(ci), hk + h * DK_A:hk + (h + 1) * DK_A]
        b_last = bh[L - 1:L, :]
        q_dec[ci, b, h] = ((qh * (DK_A ** -0.5)) * jnp.exp(bh)).astype(BF16)
        k_dec[ci, b, h] = (kh * jnp.exp(-bh)).astype(BF16)
        k_upd[ci, b, h] = (kh * jnp.exp(b_last - bh)).astype(BF16)
        decay[ci, b, h] = jnp.exp(b_last)
    att = {}
    for key in chains:
        a = lax.dot_general(q_dec[key], k_dec[key], nt, preferred_element_type=F32)
        att[key] = jnp.where(causal, a, 0.0).astype(BF16)
    outs = {}
    for (ci, b, h) in chains:
        key = (ci, b, h)
        vh = v_ref[b, rows(ci), h * DV_A:(h + 1) * DV_A]
        st = st_ref[b, h]
        outs[key] = (lax.dot_general(q_dec[key], st.astype(BF16), nt, preferred_element_type=F32)
                     + jnp.dot(att[key], vh, preferred_element_type=F32))
        st_ref[b, h] = st * decay[key] + lax.dot_general(
            vh, k_upd[key], tn_dims, preferred_element_type=F32)
    for (ci, b, h) in chains:
        vs = slice(h * DV_A, (h + 1) * DV_A)
        o = outs[ci, b, h]
        on = o * lax.rsqrt(jnp.mean(o * o, -1, keepdims=True) + LN_EPS) * gn_ref[h:h + 1, :]
        o_ref[b, rows(ci), vs] = (on * jax.nn.silu(og_ref[b, rows(ci), vs])).astype(BF16)

    @pl.when(c == n_steps - 1)
    def _():
        sfin_ref[...] = st_ref[...]


def _gla(qkg, v_a, og, wup, bg, s0_t, gn, *, B, T):
    L = min(T, CHUNK)
    n = T // L
    cps = next(c for c in (4, 2, 1) if n % c == 0)
    bb = min(B, 4)
    hv = H_A * DV_A
    seq = lambda width: pl.BlockSpec((bb, cps * L, width), lambda g, c: (g, c, 0))
    const2 = lambda g, c: (0, 0)
    state = pl.BlockSpec((bb, H_A, DV_A, DK_A), lambda g, c: (g, 0, 0, 0))
    o, s_fin = pl.pallas_call(
        functools.partial(_gla_kernel, L=L, n_steps=n // cps, bb=bb, cps=cps),
        grid=(B // bb, n // cps),
        in_specs=[seq(QKG_W), seq(hv), seq(hv),
                  pl.BlockSpec(wup.shape, const2), pl.BlockSpec(bg.shape, const2),
                  state, pl.BlockSpec(gn.shape, const2)],
        out_specs=[seq(hv), state],
        out_shape=[jax.ShapeDtypeStruct((B, T, hv), BF16),
                   jax.ShapeDtypeStruct((B, H_A, DV_A, DK_A), F32)],
        scratch_shapes=[pltpu.VMEM((bb, H_A, DV_A, DK_A), F32)],
        compiler_params=_cparams(2),
        name="gla",
    )(qkg.reshape(B, T, QKG_W), v_a.reshape(B, T, hv), og.reshape(B, T, hv), wup, bg, s0_t, gn)
    return o.reshape(B * T, hv), s_fin


def _band_attn_kernel(q_ref, kv_ref, bias_ref, o_ref):
    qb = pl.program_id(1)
    start = pl.multiple_of(jnp.maximum(qb * ATT_QB - BAND_PAST, 0), ATT_QB)
    hd = H_B * DH_B
    nt = (((1,), (1,)), ((), ()))

    def logits2(h):
        q = q_ref[:, h * DH_B:(h + 1) * DH_B]
        k = kv_ref[pl.ds(start, ATT_KW), h * DH_B:(h + 1) * DH_B]
        return (lax.dot_general(q, k, nt, preferred_element_type=F32) * (DH_B ** -0.5 * LOG2E)
                + bias_ref[0, h])

    def weights(t):
        e = jnp.exp2(t - jnp.max(t, -1, keepdims=True))
        return e.astype(BF16), jnp.sum(e, -1, keepdims=True)

    def output(h, e, den):
        v = kv_ref[pl.ds(start, ATT_KW), hd + h * DH_B:hd + (h + 1) * DH_B]
        o = jnp.dot(e, v, preferred_element_type=F32) / den
        o_ref[:, h * DH_B:(h + 1) * DH_B] = o.astype(BF16)

    t_next = logits2(0)
    for h in range(H_B):
        t_cur = t_next
        if h + 1 < H_B:
            t_next = logits2(h + 1)
        output(h, *weights(t_cur))


def _band_attn(q_b, kv_b, bias_blocks, *, B, T):
    assert T % ATT_QB == 0 and T >= ATT_KW
    n = T // ATT_QB
    hd = H_B * DH_B
    last_bias = bias_blocks.shape[0] - 1
    return pl.pallas_call(
        _band_attn_kernel,
        grid=(B, n),
        in_specs=[
            pl.BlockSpec((ATT_QB, hd), lambda b, c: (b * n + c, 0)),
            pl.BlockSpec((T, 2 * hd), lambda b, c: (b, 0)),
            pl.BlockSpec((1, H_B, ATT_QB, ATT_KW), lambda b, c: (jnp.minimum(c, last_bias), 0, 0, 0)),
        ],
        out_specs=pl.BlockSpec((ATT_QB, hd), lambda b, c: (b * n + c, 0)),
        out_shape=jax.ShapeDtypeStruct((B * T, hd), BF16),
        compiler_params=_cparams(2),
        name="band_attn",
    )(q_b, kv_b, bias_blocks)


def _attn_step_kernel(q_ref, kv_ref, ck_ref, cv_ref, bias_c_ref, bias_n_ref, o_ref):
    hd = H_B * DH_B
    lc = ck_ref.shape[0] // H_B
    nt = (((1,), (1,)), ((), ()))
    scale = DH_B ** -0.5
    def scores(h):
        hs = slice(h * DH_B, (h + 1) * DH_B)
        q = q_ref[:, hs].astype(BF16)
        kc = ck_ref[pl.ds(h, lc, stride=H_B), :].astype(BF16)
        kn = kv_ref[:, hs].astype(BF16)
        sc = lax.dot_general(q, kc, nt, preferred_element_type=F32) * scale + bias_c_ref[h]
        sn = lax.dot_general(q, kn, nt, preferred_element_type=F32) * scale + bias_n_ref[h]
        return sc, sn

    def output(h, sc, sn):
        vc = cv_ref[pl.ds(h, lc, stride=H_B), :].astype(BF16)
        vn = kv_ref[:, hd + h * DH_B:hd + (h + 1) * DH_B].astype(BF16)
        m = jnp.maximum(jnp.max(sc, -1, keepdims=True), jnp.max(sn, -1, keepdims=True))
        ec = jnp.exp(sc - m)
        en = jnp.exp(sn - m)
        den = jnp.sum(ec, -1, keepdims=True) + jnp.sum(en, -1, keepdims=True)
        o = (jnp.dot((ec / den).astype(BF16), vc, preferred_element_type=F32)
             + jnp.dot((en / den).astype(BF16), vn, preferred_element_type=F32))
        o_ref[:, h * DH_B:(h + 1) * DH_B] = o.astype(BF16)

    s_next = scores(0)
    for h in range(H_B):
        s_cur = s_next
        if h + 1 < H_B:
            s_next = scores(h + 1)
        output(h, *s_cur)


def _attn_step(q_b, kv_b, cache_k, cache_v, bias_c, bias_n, *, B, T):
    hd = H_B * DH_B
    lc = cache_k.shape[2]
    ck = cache_k.reshape(1, B, lc * H_B, DH_B)
    cv = cache_v.reshape(1, B, lc * H_B, DH_B)
    cache = pl.BlockSpec((None, None, lc * H_B, DH_B), lambda b: (0, b, 0, 0))
    return pl.pallas_call(
        _attn_step_kernel,
        grid=(B,),
        in_specs=[
            pl.BlockSpec((T, hd), lambda b: (b, 0)),
            pl.BlockSpec((T, 2 * hd), lambda b: (b, 0)),
            cache, cache,
            pl.BlockSpec(bias_c.shape, lambda b: (0, 0, 0)),
            pl.BlockSpec(bias_n.shape, lambda b: (0, 0, 0)),
        ],
        out_specs=pl.BlockSpec((T, hd), lambda b: (b, 0)),
        out_shape=jax.ShapeDtypeStruct((B * T, hd), BF16),
        compiler_params=_cparams(1),
        name="attn_step",
    )(q_b, kv_b, ck, cv, bias_c, bias_n)


def _gmlp_kernel(xb_ref, x_ref, win_ref, lvg_ref, lvb_ref, ws_ref, bs_ref, wout_ref, g_ref, b_ref,
                 y_ref, yb_ref, *maybe_v_ref, tm, L):
    row = lax.broadcasted_iota(jnp.int32, (L, L), 0)
    col = lax.broadcasted_iota(jnp.int32, (L, L), 1)
    causal = row >= col
    gw = DC // G_C
    wms = [jnp.where(causal, ws_ref[g], 0.0).astype(BF16) for g in range(G_C)]
    sub = max(ROW_SUB, L)
    n_sub = tm // sub
    mm_in = lambda r: jnp.dot(xb_ref[r * sub:(r + 1) * sub, :], win_ref[...], preferred_element_type=F32)
    uv_next = mm_in(0)
    for r in range(n_sub):
        rows = slice(r * sub, (r + 1) * sub)
        uv_raw = uv_next
        if r + 1 < n_sub:
            uv_next = mm_in(r + 1)
        uv = jax.nn.gelu(uv_raw)
        u = uv[:, :DC]
        v = _layer_norm(uv[:, DC:], lvg_ref[...], lvb_ref[...])
        if maybe_v_ref:
            maybe_v_ref[0][rows, :] = v
        vb = v.astype(BF16)
        chunks = []
        for c in range(sub // L):
            cr = slice(c * L, (c + 1) * L)
            groups = []
            for g in range(G_C):
                cs = slice(g * gw, (g + 1) * gw)
                sv = jnp.dot(wms[g], vb[cr, cs], preferred_element_type=F32) + bs_ref[:, g:g + 1]
                groups.append((u[cr, cs] * sv).astype(BF16))
            chunks.append(jnp.concatenate(groups, axis=1))
        gated = chunks[0] if len(chunks) == 1 else jnp.concatenate(chunks, axis=0)
        h = jnp.dot(gated, wout_ref[...], preferred_element_type=F32)
        y = _layer_norm(ALPHA * x_ref[rows, :] + h, g_ref[...], b_ref[...])
        y_ref[rows, :] = y
        yb_ref[rows, :] = y.astype(BF16)


def _gmlp(xb, x, O, g, b, *, T, tm, emit_v):
    m = x.shape[0]
    L = min(T, CHUNK_C)
    ws = O["ws"][:, :L, :L]
    bs_t = jnp.transpose(O["bs"][:, :L])
    row = pl.BlockSpec((tm, D_MODEL), lambda i: (i, 0))
    vec = pl.BlockSpec((1, D_MODEL), lambda i: (0, 0))
    return pl.pallas_call(
        functools.partial(_gmlp_kernel, tm=tm, L=L),
        grid=(m // tm,),
        in_specs=[row, row, _resident(O["w_in"].shape), vec, vec,
                  pl.BlockSpec(ws.shape, lambda i: (0, 0, 0)), pl.BlockSpec(bs_t.shape, lambda i: (0, 0)),
                  _resident(O["w_out"].shape), vec, vec],
        out_specs=[row] * (3 if emit_v else 2),
        out_shape=[jax.ShapeDtypeStruct((m, D_MODEL), F32), jax.ShapeDtypeStruct((m, D_MODEL), BF16)]
        + ([jax.ShapeDtypeStruct((m, DC), F32)] if emit_v else []),
        compiler_params=_cparams(1),
        name="gmlp",
    )(xb, x, O["w_in"], O["ln_v_g"].reshape(1, DC), O["ln_v_b"].reshape(1, DC), ws, bs_t,
      O["w_out"], g.reshape(1, D_MODEL), b.reshape(1, D_MODEL))


def _ffn_up_kernel(x_ref, w1_ref, w2_ref, hist_ref, cw_ref, cb_ref, h_ref, st_ref,
                   wb_ref, carry_ref, *, tm, seg, tiles_per_seq):
    j = pl.program_id(0)
    i = pl.program_id(1)
    tn = FF_TILE

    @pl.when(i == 0)
    def _():
        keep = j * tn + lax.broadcasted_iota(jnp.int32, (1, tn), 1) < D_FF
        wb_ref[:, :tn] = jnp.where(keep, w1_ref[...], 0.0).astype(BF16)
        wb_ref[:, tn:] = jnp.where(keep, w2_ref[...], 0.0).astype(BF16)
        carry_ref[...] = jnp.zeros_like(carry_ref)

    n_seg = tm // seg
    rg = min(tm, 256)
    piece = min(seg, rg)
    seq_start = i % tiles_per_seq == 0
    sub_i = lax.broadcasted_iota(jnp.int32, (SUBLANE, MXU_N), 0)
    for c in range(tn // MXU_N):
        cs = slice(c * MXU_N, (c + 1) * MXU_N)
        prev = None
        for r in range(tm // rg):
            x = x_ref[r * rg:(r + 1) * rg, :]
            u = jnp.dot(x, wb_ref[:, c * MXU_N:(c + 1) * MXU_N], preferred_element_type=F32)
            z = jnp.dot(x, wb_ref[:, tn + c * MXU_N:tn + (c + 1) * MXU_N], preferred_element_type=F32)
            for p in range(rg // piece):
                row0 = r * rg + p * piece
                s = row0 // seg
                if row0 % seg == 0:
                    if n_seg == 1:
                        prev = jnp.where(seq_start, hist_ref[0, :, cs], carry_ref[:, cs])
                    else:
                        prev = hist_ref[s, :, cs]
                up = u[p * piece:(p + 1) * piece]
                zp = z[p * piece:(p + 1) * piece]
                u1 = pltpu.roll(up, 1, 0)
                u2 = pltpu.roll(up, 2, 0)
                h1 = jnp.where(sub_i < 1, pltpu.roll(prev, 1, 0), u1[:SUBLANE])
                h2 = jnp.where(sub_i < 2, pltpu.roll(prev, 2, 0), u2[:SUBLANE])
                u1 = jnp.concatenate([h1, u1[SUBLANE:]], axis=0)
                u2 = jnp.concatenate([h2, u2[SUBLANE:]], axis=0)
                cv = u2 * cw_ref[0:1, cs] + u1 * cw_ref[1:2, cs] + up * cw_ref[2:3, cs]
                h_ref[row0:row0 + piece, cs] = (jax.nn.gelu(cv + cb_ref[:, cs]) * zp).astype(BF16)
                prev = up[piece - SUBLANE:]
                if (row0 + piece) % seg == 0:
                    st_ref[s, :, cs] = prev
                    if n_seg == 1:
                        carry_ref[:, cs] = prev


def _ffn_up(xb, w1_all, w2_all, layer, hist8, cw, cb, *, T, tm):
    m = xb.shape[0]
    tn = FF_TILE
    seg = min(T, tm)
    n_seg = tm // seg
    tps = T // seg
    wspec = pl.BlockSpec((None, D_MODEL, tn), lambda j, i: (layer, 0, j))
    return pl.pallas_call(
        functools.partial(_ffn_up_kernel, tm=tm, seg=seg, tiles_per_seq=tps),
        grid=(N_FF_TILES, m // tm),
        in_specs=[
            pl.BlockSpec((tm, D_MODEL), lambda j, i: (i, 0)),
            wspec, wspec,
            pl.BlockSpec((n_seg, SUBLANE, tn), lambda j, i: (i // tps, 0, j)),
            pl.BlockSpec((CONV_W, tn), lambda j, i: (0, j)),
            pl.BlockSpec((1, tn), lambda j, i: (0, j)),
        ],
        out_specs=[
            pl.BlockSpec((tm, tn), lambda j, i: (i, j)),
            pl.BlockSpec((n_seg, SUBLANE, tn), lambda j, i: (i, 0, j)),
        ],
        out_shape=[
            jax.ShapeDtypeStruct((m, D_FF_PAD), BF16),
            jax.ShapeDtypeStruct((m // seg, SUBLANE, D_FF_PAD), F32),
        ],
        scratch_shapes=[pltpu.VMEM((D_MODEL, 2 * tn), BF16), pltpu.VMEM((SUBLANE, tn), F32)],
        compiler_params=_cparams(2),
        name="ffn_up",
    )(xb, w1_all, w2_all, hist8, cw, cb)


def _ffn_down_kernel(h_ref, w_ref, res_ref, g_ref, b_ref, y_ref, yb_ref, *, tm):
    for r in range(tm // ROW_SUB):
        rows = slice(r * ROW_SUB, (r + 1) * ROW_SUB)
        acc = jnp.dot(h_ref[rows, :], w_ref[...], preferred_element_type=F32)
        y = _layer_norm(ALPHA * res_ref[rows, :] + acc, g_ref[...], b_ref[...])
        y_ref[rows, :] = y
        yb_ref[rows, :] = y.astype(BF16)


def _ffn_down(h, w3, res, g, b, *, tm):
    m, n = res.shape
    row = lambda width: pl.BlockSpec((tm, width), lambda i: (i, 0))
    vec = pl.BlockSpec((1, n), lambda i: (0, 0))
    return pl.pallas_call(
        functools.partial(_ffn_down_kernel, tm=tm),
        grid=(m // tm,),
        in_specs=[row(D_FF_PAD), _resident(w3.shape), row(n), vec, vec],
        out_specs=[row(n), row(n)],
        out_shape=[jax.ShapeDtypeStruct((m, n), F32), jax.ShapeDtypeStruct((m, n), BF16)],
        compiler_params=_cparams(1, vmem=VMEM_LIMIT_FFN_DOWN),
        name="ffn_down",
    )(h, w3, res, g.reshape(1, n), b.reshape(1, n))


def _cast_pad_rows_kernel(w_ref, o_ref, *, rows_valid, tr):
    k = pl.program_id(0)
    keep = k * tr + lax.broadcasted_iota(jnp.int32, (tr, 1), 0) < rows_valid
    o_ref[...] = jnp.where(keep, w_ref[...], 0.0).astype(BF16)


def _cast_pad_rows(w_all, layer, rows_pad, *, tr=FF_TILE):
    _, rows, n = w_all.shape
    return pl.pallas_call(
        functools.partial(_cast_pad_rows_kernel, rows_valid=rows, tr=tr),
        grid=(rows_pad // tr,),
        in_specs=[pl.BlockSpec((None, tr, n), lambda k: (layer, k, 0))],
        out_specs=pl.BlockSpec((tr, n), lambda k: (k, 0)),
        out_shape=jax.ShapeDtypeStruct((rows_pad, n), BF16),
        compiler_params=_cparams(1),
        name="cast_pad_rows",
    )(w_all)


def _split_w_in_kernel(w_ref, qkg_ref, v_ref, og_ref, qb_ref, kv_ref):
    hk, hv, hd = H_A * DK_A, H_A * DV_A, H_B * DH_B
    qkg_ref[:, :2 * hk] = w_ref[:, :2 * hk].astype(BF16)
    o = 2 * hk
    v_ref[...] = w_ref[:, o:o + hv].astype(BF16)
    o += hv
    g = w_ref[:, o:o + GATE_PAD]
    lane = lax.broadcasted_iota(jnp.int32, g.shape, 1)
    qkg_ref[:, 2 * hk:] = jnp.where(lane < GATE_RANK, g, 0.0).astype(BF16)
    o += GATE_RANK
    og_ref[...] = w_ref[:, o:o + hv].astype(BF16)
    o += hv
    qb_ref[...] = w_ref[:, o:o + hd].astype(BF16)
    o += hd
    kv_ref[...] = w_ref[:, o:o + 2 * hd].astype(BF16)


def _split_w_in(w_in, *, tr=256):
    k, n = w_in.shape
    hv, hd = H_A * DV_A, H_B * DH_B
    widths = [QKG_W, hv, hv, hd, 2 * hd]
    return pl.pallas_call(
        _split_w_in_kernel,
        grid=(k // tr,),
        in_specs=[pl.BlockSpec((tr, n), lambda i: (i, 0))],
        out_specs=[pl.BlockSpec((tr, w), lambda i: (i, 0)) for w in widths],
        out_shape=[jax.ShapeDtypeStruct((k, w), BF16) for w in widths],
        compiler_params=_cparams(1),
        name="split_w_in",
    )(w_in)


def _prep_even(w_in, w_gate_up, b_gate, w_out):
    hk, hv = H_A * DK_A, H_A * DV_A
    w_qkg, w_v, w_og, w_qb, w_kv = _split_w_in(w_in)
    wup = jnp.pad(w_gate_up, ((0, GATE_PAD - GATE_RANK), (0, 0)))
    return dict(
        w_qkg=w_qkg, w_v=w_v, w_og=w_og, w_qb=w_qb, w_kv=w_kv,
        wup=wup.astype(BF16), bg=b_gate.reshape(1, hk),
        w_out_a=w_out[:hv].astype(BF16), w_out_b=w_out[hv:].astype(BF16),
    )


def _prep_ffn(cw, cb, w3_all, layer):
    pad = D_FF_PAD - D_FF
    return dict(layer=layer,
                cw=jnp.pad(cw, ((0, 0), (0, pad))),
                cb=jnp.pad(cb, (0, pad)).reshape(1, D_FF_PAD),
                w3=_cast_pad_rows(w3_all, layer, D_FF_PAD))


def _toeplitz_kernel(ext_ref, o_ref):
    heads, rows, width = o_ref.shape
    for h in range(heads):
        o_ref[h] = pltpu.roll(jnp.broadcast_to(ext_ref[h], (rows, width)), 0, 1, stride=1, stride_axis=0)


def _rel_toeplitz(table, n_rows, n_cols, offset):
    heads = table.shape[0]
    pad = -(-n_rows // LANE) * LANE
    width = -(-(pad + n_cols) // LANE) * LANE
    n_hi = offset + pad - REL_CLIP
    assert n_hi >= 0
    ext = jnp.concatenate([
        jnp.broadcast_to(table[:, -1:], (heads, n_hi)), table[:, ::-1],
        jnp.broadcast_to(table[:, :1], (heads, width))], axis=1)[:, :width]
    toe = pl.pallas_call(
        _toeplitz_kernel,
        grid=(1,),
        in_specs=[pl.BlockSpec((heads, 1, width), lambda i: (0, 0, 0))],
        out_specs=pl.BlockSpec((heads, n_rows, width), lambda i: (0, 0, 0)),
        out_shape=jax.ShapeDtypeStruct((heads, n_rows, width), F32),
        compiler_params=_cparams(1),
        name="rel_toeplitz",
    )(ext.astype(F32).reshape(heads, 1, width))
    return toe, pad


def _prompt_bias(table):
    n_special = BAND_PAST // ATT_QB
    toe, pad = _rel_toeplitz(table, ATT_QB, ATT_KW + BAND_PAST, BAND_PAST)
    r = jnp.arange(ATT_QB)[:, None]
    j = jnp.arange(ATT_KW)[None, :]
    out = []
    for qb in range(n_special + 1):
        q_chunk = (qb * ATT_QB + r) // CHUNK
        valid = (j >= q_chunk * CHUNK - BAND_PAST) & (j < (q_chunk + 1) * CHUNK)
        shift = pad + (n_special - qb) * ATT_QB
        out.append(jnp.where(valid[None], toe[:, :, shift:shift + ATT_KW] * LOG2E, NEG_INF))
    return jnp.stack(out)


def _step_bias(table, T, lc):
    toe, pad = _rel_toeplitz(table, T, lc + T, lc)
    return toe[:, :, pad:pad + lc], toe[:, :, pad + lc:pad + lc + T]


def _conv_ffn(x, xb, F, ln_g, ln_b, hist, *, B, T, tm_up, tm_down):
    pad = D_FF_PAD - D_FF
    keep = CONV_W - 1
    hist8 = jnp.pad(hist, ((0, 0), (SUBLANE - keep, 0), (0, pad)))
    h, st8 = _ffn_up(xb, F["w1_all"], F["w2_all"], F["layer"], hist8, F["cw"], F["cb"], T=T, tm=tm_up)
    new_state = st8.reshape(B, -1, SUBLANE, D_FF_PAD)[:, -1, SUBLANE - keep:, :D_FF]
    y, yb = _ffn_down(h, F["w3"], x, ln_g, ln_b, tm=tm_down)
    return y, yb, new_state


def _trunk(x3, E, Fs, O, norms, state, bias):
    B, T, _ = x3.shape
    m = B * T
    tm = min(512, m)
    tm_lin = min(1024, m)
    tm_up = min(2048, m)
    tm_small = min(256, m)
    x = x3.reshape(m, D_MODEL)
    hv, hd = H_A * DV_A, H_B * DH_B
    prompt = state is None

    qkg, xb = _linear(x, E["w_qkg"], out_dtype=F32, tm=tm_lin, emit_xb=True)
    v_a = _linear(xb, E["w_v"], out_dtype=BF16, tm=tm_lin)
    og = _linear(xb, E["w_og"], out_dtype=F32, tm=tm_lin)
    q_b = _linear(xb, E["w_qb"], out_dtype=BF16, tm=tm_lin)
    if prompt:
        s0_t = jnp.zeros((B, H_A, DV_A, DK_A), F32)
        conv_hist = [jnp.zeros((B, CONV_W - 1, D_FF), F32)] * DEPTH
    else:
        cache_k, cache_v, state_gla, state_conv = state
        s0_t = jnp.swapaxes(state_gla[0], -1, -2)
        conv_hist = [state_conv[i] for i in range(DEPTH)]
    o_a, s_fin_t = _gla(qkg, v_a, og, E["wup"], E["bg"], s0_t, E["gn"], B=B, T=T)
    new_gla = jnp.swapaxes(s_fin_t, -1, -2)[None]
    if prompt:
        keep = min(BAND_PAST, T)
        kv_b, kv_tail = _linear(xb, E["w_kv"], out_dtype=BF16, tm=tm_lin, tail=(T // tm_lin, keep))
        o_b = _band_attn(q_b, kv_b, bias, B=B, T=T)
        kv_rows = kv_tail.reshape(B, keep, 2, H_B, DH_B)
    else:
        kv_f = _linear(xb, E["w_kv"], out_dtype=F32, tm=tm_lin)
        o_b = _attn_step(q_b, kv_f, cache_k, cache_v, bias[0], bias[1], B=B, T=T)
        kv_rows = kv_f.reshape(B, T, 2, H_B, DH_B)
    new_k = kv_rows[:, :, 0][None]
    new_v = kv_rows[:, :, 1][None]
    x, xb = _proj_ln(o_a, o_b, E["w_out_a"], E["w_out_b"], x, norms["ln1_g"][0], norms["ln1_b"][0], tm=tm)
    x, xb, conv0 = _conv_ffn(x, xb, Fs[0], norms["ln2_g"][0], norms["ln2_b"][0], conv_hist[0],
                             B=B, T=T, tm_up=tm_up, tm_down=tm)

    x, xb, *maybe_v = _gmlp(xb, x, O, norms["ln1_g"][1], norms["ln1_b"][1], T=T,
                            tm=tm_small, emit_v=not prompt)
    x, xb, conv1 = _conv_ffn(x, xb, Fs[1], norms["ln2_g"][1], norms["ln2_b"][1], conv_hist[1],
                             B=B, T=T, tm_up=tm_up, tm_down=tm)

    y = x.reshape(B, T, D_MODEL)
    new_conv = jnp.stack([conv0, conv1])
    mlp_v = maybe_v[0].reshape(B, T, DC)[None] if maybe_v else None
    return y, new_k, new_v, new_gla, new_conv, mlp_v


def kernel(x_prompt, x_sample, cache_attn_k, cache_attn_v, state_gla, state_ffn_conv, w_in_even, w_gate_up, b_gate, gla_norm_g, rel_bias, w_out_even, w_in_odd, ln_v_g, ln_v_b, w_spatial, b_spatial, w_out_odd, ffn_w1, ffn_w2, ffn_conv_w, ffn_conv_b, ffn_w3, ln1_g, ln1_b, ln2_g, ln2_b):
    E = _prep_even(w_in_even.reshape(w_in_even.shape[-2:]), w_gate_up[0], b_gate[0], w_out_even[0])
    E["gn"] = gla_norm_g[0]
    Fs = [dict(_prep_ffn(ffn_conv_w[i], ffn_conv_b[i], ffn_w3, i), w1_all=ffn_w1, w2_all=ffn_w2)
          for i in range(DEPTH)]
    O = dict(w_in=w_in_odd[0].astype(BF16), ln_v_g=ln_v_g[0], ln_v_b=ln_v_b[0],
             ws=w_spatial[0], bs=b_spatial[0], w_out=w_out_odd[0].astype(BF16))
    norms = dict(ln1_g=ln1_g, ln1_b=ln1_b, ln2_g=ln2_g, ln2_b=ln2_b)

    bias_p = _prompt_bias(rel_bias[0])
    y_p, k_p, v_p, gla_p, conv_p, _ = _trunk(x_prompt, E, Fs, O, norms, None, bias_p)

    bias_s = _step_bias(rel_bias[0], x_sample.shape[1], cache_attn_k.shape[2])
    y_s, k_s, v_s, gla_s, conv_s, mlp_v_s = _trunk(
        x_sample, E, Fs, O, norms, (cache_attn_k, cache_attn_v, state_gla, state_ffn_conv), bias_s)
    return (y_p, y_s, k_p, v_p, gla_p, conv_p, k_s, v_s, gla_s, conv_s, mlp_v_s)
```
